```python
import math
import jax, jax.numpy as jnp
from jax import lax
import numpy as np

D_MODEL = 1024
BATCH = 2
SEQ = 8192
DEPTH = 1

N_META = 16
N_HEADS = 8
HEAD_DIM = 64
V_DIM = 2 * HEAD_DIM
QK_WIDTH = N_HEADS * 2 * HEAD_DIM
ATTN_WIDTH = N_HEADS * V_DIM
POOL_GROUPS = 4
POOL_WINDOWS = (2, 4, 8, 16)
POOL_WIDTH = 512
POOL_GDIM = POOL_WIDTH // POOL_GROUPS
N_BRANCH = 2
IN_COLS = 2 * QK_WIDTH + ATTN_WIDTH + POOL_WIDTH + N_BRANCH * D_MODEL
D_FF = 2816
CONV_WIDTH = 3
ROPE_THETA = 10000.0
EPS = 1e-6
Q_BLOCK = 128

kernel_name = "hybrid_diffattn_pool_convffn"


def rmsnorm(x, g):
    xf = x.astype(jnp.float32)
    r = lax.rsqrt(jnp.mean(xf * xf, axis=-1, keepdims=True) + EPS)
    return (xf * r).astype(x.dtype) * g


def rope_tables(length):
    pos = jnp.arange(length, dtype=jnp.float32)
    inv = 1.0 / (ROPE_THETA ** (jnp.arange(0, HEAD_DIM, 2, dtype=jnp.float32) / HEAD_DIM))
    ang = pos[:, None] * inv[None, :]
    return jnp.cos(ang), jnp.sin(ang)


def apply_rope(t, cos, sin):
    c = cos[None, :, None, None, :].astype(t.dtype)
    s = sin[None, :, None, None, :].astype(t.dtype)
    t1, t2 = jnp.split(t, 2, axis=-1)
    return jnp.concatenate([t1 * c - t2 * s, t2 * c + t1 * s], axis=-1)


def diff_attention(q, k, v, lam_full, g_subln, lam_init):
    B, L = q.shape[0], q.shape[1]
    nblk = L // Q_BLOCK
    scale = 1.0 / math.sqrt(HEAD_DIM)
    qb = q.reshape(B, nblk, Q_BLOCK, N_HEADS, 2, HEAD_DIM).transpose(1, 0, 2, 3, 4, 5)
    key_pos = jnp.arange(L)

    def block(args):
        qi, i = args
        s = jnp.einsum('bqhcd,bkhcd->bhcqk', qi, k,
                       preferred_element_type=jnp.float32) * scale
        qpos = i * Q_BLOCK + jnp.arange(Q_BLOCK)
        mask = key_pos[None, :] <= qpos[:, None]
        s = jnp.where(mask[None, None, None], s, -jnp.inf)
        p = jax.nn.softmax(s, axis=-1)
        a = p[:, :, 0] - lam_full * p[:, :, 1]
        return jnp.einsum('bhqk,bkhe->bqhe', a.astype(v.dtype), v)

    out = lax.map(block, (qb, jnp.arange(nblk)))
    out = out.transpose(1, 0, 2, 3, 4).reshape(B, L, N_HEADS, V_DIM)
    out = rmsnorm(out, g_subln) * (1.0 - lam_init)
    return out.reshape(B, L, ATTN_WIDTH)


def multiscale_pool(u, w_grp, scale):
    B, L = u.shape[0], u.shape[1]
    uf = u.astype(jnp.float32).reshape(B, L, POOL_GROUPS, POOL_GDIM)
    cs = jnp.concatenate([jnp.zeros((B, 1, POOL_GROUPS, POOL_GDIM), jnp.float32),
                          jnp.cumsum(uf, axis=1)], axis=1)
    t = jnp.arange(L)
    win = jnp.array(POOL_WINDOWS, dtype=jnp.int32)
    lo = jnp.maximum(t[:, None] + 1 - win[None, :], 0)
    gidx = jnp.broadcast_to(jnp.arange(POOL_GROUPS)[None, :], lo.shape)
    sums = cs[:, 1:] - cs[:, lo, gidx]
    count = (t[:, None] + 1 - lo).astype(jnp.float32)
    pooled = (sums / count[None, :, :, None] - uf).astype(u.dtype)
    mixed = jnp.einsum('blgc,gcd->blgd', pooled, w_grp)
    return mixed.reshape(B, L, POOL_WIDTH) * scale


def causal_dwconv(u, w, b):
    C = u.shape[-1]
    y = lax.conv_general_dilated(u, w.reshape(CONV_WIDTH, 1, C).astype(u.dtype),
                                 window_strides=(1,), padding=[(CONV_WIDTH - 1, 0)],
                                 dimension_numbers=('NWC', 'WIO', 'NWC'),
                                 feature_group_count=C)
    return y + b


def setup_inputs(seed: int = 0) -> dict:
    key = jax.random.key(seed)
    ks = jax.random.split(key, 17)
    f32 = jnp.float32
    nrm = lambda k, shp, s: jax.random.normal(k, shp, f32) * s
    return {
        "x": nrm(ks[0], (BATCH, SEQ, D_MODEL), 1.0),
        "meta_tokens": nrm(ks[1], (N_META, D_MODEL), 1.0),
        "g_mix": 1.0 + nrm(ks[2], (DEPTH, D_MODEL), 0.02),
        "w_in": nrm(ks[3], (DEPTH, D_MODEL, IN_COLS), D_MODEL ** -0.5),
        "lam": nrm(ks[4], (DEPTH, 4, HEAD_DIM), 0.1),
        "g_subln": 1.0 + nrm(ks[5], (DEPTH, V_DIM), 0.02),
        "w_pool_grp": nrm(ks[6], (DEPTH, POOL_GROUPS, POOL_GDIM, POOL_GDIM), POOL_GDIM ** -0.5),
        "pool_scale": 1.0 + nrm(ks[7], (DEPTH, POOL_WIDTH), 0.02),
        "w_attn_br": nrm(ks[8], (DEPTH, ATTN_WIDTH, D_MODEL), ATTN_WIDTH ** -0.5),
        "w_pool_br": nrm(ks[9], (DEPTH, POOL_WIDTH, D_MODEL), POOL_WIDTH ** -0.5),
        "w_out": nrm(ks[10], (DEPTH, D_MODEL, D_MODEL), D_MODEL ** -0.5),
        "g_ffn": 1.0 + nrm(ks[11], (DEPTH, D_MODEL), 0.02),
        "w_up": nrm(ks[12], (DEPTH, D_MODEL, 2 * D_FF), D_MODEL ** -0.5),
        "conv_w": nrm(ks[13], (DEPTH, CONV_WIDTH, 2 * D_FF), CONV_WIDTH ** -0.5),
        "conv_b": nrm(ks[14], (DEPTH, 2 * D_FF), 0.02),
        "w_down": nrm(ks[15], (DEPTH, D_FF, D_MODEL), D_FF ** -0.5),
        "g_final": 1.0 + nrm(ks[16], (D_MODEL,), 0.02),
    }


def reference(x, meta_tokens, g_mix, w_in, lam, g_subln, w_pool_grp, pool_scale,
              w_attn_br, w_pool_br, w_out, g_ffn, w_up, conv_w, conv_b, w_down, g_final):
    B = x.shape[0]
    L = N_META + SEQ
    L_pad = ((L + Q_BLOCK - 1) // Q_BLOCK) * Q_BLOCK
    meta = jnp.broadcast_to(meta_tokens.astype(x.dtype)[None], (B, N_META, D_MODEL))
    pad = jnp.zeros((B, L_pad - L, D_MODEL), x.dtype)
    h = jnp.concatenate([meta, x, pad], axis=1)
    cos, sin = rope_tables(L_pad)
    splits = [QK_WIDTH, 2 * QK_WIDTH, 2 * QK_WIDTH + ATTN_WIDTH,
              2 * QK_WIDTH + ATTN_WIDTH + POOL_WIDTH,
              2 * QK_WIDTH + ATTN_WIDTH + POOL_WIDTH + D_MODEL]

    for layer in range(DEPTH):
        hn = rmsnorm(h, g_mix[layer])
        proj = hn @ w_in[layer]
        q, k, v, u, ga, gp = jnp.split(proj, splits, axis=-1)
        q = apply_rope(q.reshape(B, L_pad, N_HEADS, 2, HEAD_DIM), cos, sin)
        k = apply_rope(k.reshape(B, L_pad, N_HEADS, 2, HEAD_DIM), cos, sin)
        v = v.reshape(B, L_pad, N_HEADS, V_DIM)
        lam_init = 0.8 - 0.6 * math.exp(-0.3 * layer)
        lp = lam[layer].astype(jnp.float32)
        lam_full = (jnp.exp(jnp.sum(lp[0] * lp[1])) - jnp.exp(jnp.sum(lp[2] * lp[3]))
                    + lam_init)
        attn = diff_attention(q, k, v, lam_full, g_subln[layer], lam_init)
        pool = multiscale_pool(u, w_pool_grp[layer], pool_scale[layer])
        merged = (jax.nn.sigmoid(ga) * (attn @ w_attn_br[layer])
                  + jax.nn.sigmoid(gp) * (pool @ w_pool_br[layer]))
        h = h + merged @ w_out[layer]
        hn = rmsnorm(h, g_ffn[layer])
        up = causal_dwconv(hn @ w_up[layer], conv_w[layer], conv_b[layer])
        val, gate = jnp.split(up, 2, axis=-1)
        h = h + (jax.nn.silu(gate) * val) @ w_down[layer]

    h = rmsnorm(h, g_final)
    return h[:, N_META:N_META + SEQ]
```

```python
import functools
import math

import jax
import jax.numpy as jnp
from jax import lax
from jax.experimental import pallas as pl
from jax.experimental.pallas import tpu as pltpu

D_MODEL = 1024
N_META = 16
N_HEADS = 8
HEAD_DIM = 64
V_DIM = 2 * HEAD_DIM
QK_WIDTH = N_HEADS * 2 * HEAD_DIM
ATTN_WIDTH = N_HEADS * V_DIM
POOL_GROUPS = 4
POOL_WINDOWS = (2, 4, 8, 16)
POOL_WIDTH = 512
POOL_GDIM = POOL_WIDTH // POOL_GROUPS
D_FF = 2816
CONV_WIDTH = 3
ROPE_THETA = 10000.0
EPS = 1e-6
Q_BLOCK = 128
LAM_INIT = 0.8 - 0.6 * math.exp(-0.3 * 0)

COL_Q = 0
COL_K = QK_WIDTH
COL_V = 2 * QK_WIDTH
COL_U = 2 * QK_WIDTH + ATTN_WIDTH
COL_GA = COL_U + POOL_WIDTH
COL_GP = COL_GA + D_MODEL

LANES = 128
POOL_HALO = 16
CONV_HALO = 8
VMEM_LIMIT = 56 * 1024 * 1024

TM_IN = 640
TQ = 640
TK = 640
TM_OUT = 320
FF_CHUNK = 256


def _rms_scale(x):
    return lax.rsqrt(jnp.mean(x * x, axis=-1, keepdims=True) + EPS)


def _resident(shape):
    return pl.BlockSpec(shape, lambda *_: (0,) * len(shape), pipeline_mode=pl.Buffered(1))


def _mix_in_kernel(h_ref, g_ref, w_in_ref, cos_ref, sin_ref, wgrp_ref, pscale_ref, wpbr_ref,
                   q_ref, k_ref, v_ref, sga_ref, pm_ref, ubuf_ref):
    t = pl.program_id(1)
    tm = h_ref.shape[0]

    h = h_ref[...]
    hn = ((h * _rms_scale(h)) * g_ref[...]).astype(jnp.bfloat16)

    def proj(col, width):
        return jnp.dot(hn, w_in_ref[:, col:col + width], preferred_element_type=jnp.float32)

    cos = cos_ref[...]
    sin = sin_ref[...]
    lane = lax.broadcasted_iota(jnp.int32, (tm, LANES), 1)
    first_half = (lane % HEAD_DIM) < (HEAD_DIM // 2)

    def rope_store(col, out_ref, scale):
        pj = proj(col, QK_WIDTH)
        for hh in range(N_HEADS):
            xh = pj[:, hh * LANES:(hh + 1) * LANES]
            partner = jnp.where(first_half,
                                pltpu.roll(xh, LANES - HEAD_DIM // 2, axis=1),
                                pltpu.roll(xh, HEAD_DIM // 2, axis=1))
            r = xh * cos + partner * sin
            if scale != 1.0:
                r = r * scale
            out_ref[:, hh * LANES:(hh + 1) * LANES] = r.astype(out_ref.dtype)

    rope_store(COL_Q, q_ref, 1.0 / math.sqrt(HEAD_DIM))
    rope_store(COL_K, k_ref, 1.0)
    v_ref[...] = proj(COL_V, ATTN_WIDTH).astype(v_ref.dtype)
    sga_ref[...] = jax.nn.sigmoid(proj(COL_GA, D_MODEL)).astype(sga_ref.dtype)

    @pl.when(t == 0)
    def _():
        ubuf_ref[0:POOL_HALO, :] = jnp.zeros((POOL_HALO, POOL_WIDTH), jnp.float32)

    ubuf_ref[POOL_HALO:POOL_HALO + tm, :] = proj(COL_U, POOL_WIDTH)
    pos = t * tm + lax.broadcasted_iota(jnp.int32, (tm, 1), 0)
    mixed = []
    for g, w in enumerate(POOL_WINDOWS):
        cols = slice(g * POOL_GDIM, (g + 1) * POOL_GDIM)
        u_g = ubuf_ref[POOL_HALO:POOL_HALO + tm, cols]
        sums = u_g
        for j in range(1, w):
            sums = sums + ubuf_ref[POOL_HALO - j:POOL_HALO - j + tm, cols]
        count = jnp.minimum(pos + 1, w).astype(jnp.float32)
        pooled = (sums / count - u_g).astype(jnp.bfloat16)
        m_g = jnp.dot(pooled, wgrp_ref[g], preferred_element_type=jnp.float32)
        mixed.append((m_g * pscale_ref[:, cols]).astype(jnp.bfloat16))
    pool = jnp.concatenate(mixed, axis=-1)
    pbr = jnp.dot(pool, wpbr_ref[...], preferred_element_type=jnp.float32)
    pm_ref[...] = (jax.nn.sigmoid(proj(COL_GP, D_MODEL)) * pbr).astype(pm_ref.dtype)

    ubuf_ref[0:POOL_HALO, :] = ubuf_ref[tm:tm + POOL_HALO, :]


def _mix_in(h, g_mix, w_in, cos_t, sin_t, w_grp, pool_scale, w_pool_br):
    B, L, D = h.shape
    tm = TM_IN
    row = lambda width: pl.BlockSpec((None, tm, width), lambda b, t: (b, t, 0))
    out_sds = lambda dt: jax.ShapeDtypeStruct((B, L, D), dt)
    return pl.pallas_call(
        _mix_in_kernel,
        grid=(B, L // tm),
        in_specs=[
            row(D),
            _resident((1, D)),
            _resident(w_in.shape),
            pl.BlockSpec((tm, LANES), lambda b, t: (t, 0)),
            pl.BlockSpec((tm, LANES), lambda b, t: (t, 0)),
            _resident(w_grp.shape),
            _resident((1, POOL_WIDTH)),
            _resident(w_pool_br.shape),
        ],
        out_specs=[row(D)] * 5,
        out_shape=[out_sds(jnp.bfloat16), out_sds(jnp.bfloat16), out_sds(jnp.bfloat16),
                   out_sds(jnp.float32), out_sds(jnp.float32)],
        scratch_shapes=[pltpu.VMEM((tm + POOL_HALO, POOL_WIDTH), jnp.float32)],
        compiler_params=pltpu.CompilerParams(
            dimension_semantics=("arbitrary", "arbitrary"), vmem_limit_bytes=VMEM_LIMIT),
        name="mix_in",
    )(h, g_mix, w_in, cos_t, sin_t, w_grp, pool_scale, w_pool_br)


def _diffattn_kernel(lam_ref, gsub_ref, q_ref, k_ref, v_ref, o_ref, qq_ref):
    qi = pl.program_id(2)
    tq = q_ref.shape[0]

    q = q_ref[...]
    lane = lax.broadcasted_iota(jnp.int32, q.shape, 1)
    zero = jnp.zeros_like(q)
    qq_ref[0:tq, :] = jnp.where(lane < HEAD_DIM, q, zero)
    qq_ref[tq:2 * tq, :] = jnp.where(lane >= HEAD_DIM, q, zero)
    qq = qq_ref[...]

    def step(start, carry, masked):
        m, l, acc = carry
        kc = k_ref[pl.ds(start, TK), :]
        vc = v_ref[pl.ds(start, TK), :]
        s = lax.dot_general(qq, kc, (((1,), (1,)), ((), ())), preferred_element_type=jnp.float32)
        if masked:
            r = lax.broadcasted_iota(jnp.int32, (2 * tq, TK), 0)
            c = lax.broadcasted_iota(jnp.int32, (2 * tq, TK), 1)
            r = jnp.where(r >= tq, r - tq, r)
            s = jnp.where(c <= r, s, -jnp.inf)
        m_new = jnp.maximum(m, jnp.max(s, axis=-1, keepdims=True))
        alpha = jnp.exp(m - m_new)
        p = jnp.exp(s - m_new)
        l = alpha * l + jnp.sum(p, axis=-1, keepdims=True)
        acc = alpha * acc + jnp.dot(p.astype(vc.dtype), vc, preferred_element_type=jnp.float32)
        return m_new, l, acc

    init = (jnp.full((2 * tq, 1), -jnp.inf, jnp.float32),
            jnp.zeros((2 * tq, 1), jnp.float32),
            jnp.zeros((2 * tq, V_DIM), jnp.float32))
    carry = lax.fori_loop(0, qi, lambda i, c: step(pl.multiple_of(i * TK, TK), c, False), init)
    _, l, acc = step(pl.multiple_of(qi * TK, TK), carry, True)

    lam = lam_ref[...]
    lam_full = (jnp.exp(jnp.sum(lam[0:1, :] * lam[1:2, :], axis=-1, keepdims=True))
                - jnp.exp(jnp.sum(lam[2:3, :] * lam[3:4, :], axis=-1, keepdims=True)) + LAM_INIT)
    o = acc[0:tq, :] / l[0:tq, :] - lam_full * (acc[tq:2 * tq, :] / l[tq:2 * tq, :])
    o = (o * _rms_scale(o)) * gsub_ref[...] * (1.0 - LAM_INIT)
    o_ref[...] = o.astype(o_ref.dtype)


def _diffattn(lam, g_subln, q, k, v):
    B, L, _ = q.shape
    assert TQ == TK and L % TQ == 0
    kv_spec = pl.BlockSpec((None, L, V_DIM), lambda b, h, i: (b, 0, h))
    return pl.pallas_call(
        _diffattn_kernel,
        grid=(B, N_HEADS, L // TQ),
        in_specs=[
            _resident(lam.shape),
            _resident((1, V_DIM)),
            pl.BlockSpec((None, TQ, V_DIM), lambda b, h, i: (b, i, h)),
            kv_spec,
            kv_spec,
        ],
        out_specs=pl.BlockSpec((None, TQ, V_DIM), lambda b, h, i: (b, i, h)),
        out_shape=jax.ShapeDtypeStruct((B, L, ATTN_WIDTH), jnp.bfloat16),
        scratch_shapes=[pltpu.VMEM((2 * TQ, V_DIM), jnp.bfloat16)],
        compiler_params=pltpu.CompilerParams(
            dimension_semantics=("arbitrary", "arbitrary", "arbitrary"), vmem_limit_bytes=VMEM_LIMIT),
        name="diffattn",
    )(lam, g_subln, q, k, v)


def _mix_out_kernel(h_ref, attn_ref, sga_ref, pm_ref, wabr_ref, wout_ref, gffn_ref, wup_ref,
                    convw_ref, convb_ref, wdown_ref, gfin_ref, o_ref, cbuf_ref, halo_ref):
    t = pl.program_id(1)
    tm = h_ref.shape[0]
    ch = FF_CHUNK

    abr = jnp.dot(attn_ref[...], wabr_ref[...], preferred_element_type=jnp.float32)
    merged = sga_ref[...] * abr + pm_ref[...]
    h1 = h_ref[...] + jnp.dot(merged.astype(jnp.bfloat16), wout_ref[...],
                              preferred_element_type=jnp.float32)
    hn = ((h1 * _rms_scale(h1)) * gffn_ref[...]).astype(jnp.bfloat16)

    @pl.when(t == 0)
    def _():
        halo_ref[...] = jnp.zeros_like(halo_ref)

    def conv(col, c):
        up = jnp.dot(hn, wup_ref[:, col:col + ch], preferred_element_type=jnp.float32)
        cbuf_ref[0:CONV_HALO, :] = halo_ref[:, col:col + ch]
        cbuf_ref[CONV_HALO:CONV_HALO + tm, :] = up
        halo_ref[:, col:col + ch] = cbuf_ref[tm:tm + CONV_HALO, :]
        y = convb_ref[:, col:col + ch] + convw_ref[CONV_WIDTH - 1:CONV_WIDTH, col:col + ch] * up
        for j in range(1, CONV_WIDTH):
            wj = convw_ref[CONV_WIDTH - 1 - j:CONV_WIDTH - j, col:col + ch]
            y = y + wj * cbuf_ref[CONV_HALO - j:CONV_HALO - j + tm, :]
        return y

    acc = jnp.zeros((tm, D_MODEL), jnp.float32)
    for c in range(D_FF // ch):
        val = conv(c * ch, c)
        gate = conv(D_FF + c * ch, c)
        act = (jax.nn.silu(gate) * val).astype(jnp.bfloat16)
        acc = acc + jnp.dot(act, wdown_ref[c * ch:(c + 1) * ch, :], preferred_element_type=jnp.float32)

    h2 = h1 + acc
    o_ref[...] = ((h2 * _rms_scale(h2)) * gfin_ref[...]).astype(o_ref.dtype)


def _mix_out(h, attn, sga, pm, w_attn_br, w_out, g_ffn, w_up, conv_w, conv_b, w_down, g_final):
    B, L, D = h.shape
    tm = TM_OUT
    assert D_FF % FF_CHUNK == 0 and L % tm == 0
    row = pl.BlockSpec((None, tm, D), lambda b, t: (b, t, 0))
    return pl.pallas_call(
        _mix_out_kernel,
        grid=(B, L // tm),
        in_specs=[
            row, row, row, row,
            _resident(w_attn_br.shape),
            _resident(w_out.shape),
            _resident((1, D)),
            _resident(w_up.shape),
            _resident(conv_w.shape),
            _resident(conv_b.shape),
            _resident(w_down.shape),
            _resident((1, D)),
        ],
        out_specs=row,
        out_shape=jax.ShapeDtypeStruct((B, L, D), jnp.float32),
        scratch_shapes=[pltpu.VMEM((tm + CONV_HALO, FF_CHUNK), jnp.float32),
                        pltpu.VMEM((CONV_HALO, 2 * D_FF), jnp.float32)],
        compiler_params=pltpu.CompilerParams(
            dimension_semantics=("arbitrary", "arbitrary"), vmem_limit_bytes=VMEM_LIMIT),
        name="mix_out",
    )(h, attn, sga, pm, w_attn_br, w_out, g_ffn, w_up, conv_w, conv_b, w_down, g_final)


def _rope_tables(length):
    pos = jnp.arange(length, dtype=jnp.float32)
    inv = 1.0 / (ROPE_THETA ** (jnp.arange(0, HEAD_DIM, 2, dtype=jnp.float32) / HEAD_DIM))
    ang = pos[:, None] * inv[None, :]
    cos, sin = jnp.cos(ang), jnp.sin(ang)
    cos_t = jnp.concatenate([cos, cos, cos, cos], axis=-1)
    sin_t = jnp.concatenate([-sin, sin, -sin, sin], axis=-1)
    return cos_t, sin_t


def kernel(x, meta_tokens, g_mix, w_in, lam, g_subln, w_pool_grp, pool_scale, w_attn_br, w_pool_br,
           w_out, g_ffn, w_up, conv_w, conv_b, w_down, g_final):
    B, S, D = x.shape
    L = N_META + S
    L_pad = ((L + Q_BLOCK - 1) // Q_BLOCK) * Q_BLOCK
    bf = jnp.bfloat16
    meta = jnp.broadcast_to(meta_tokens.astype(x.dtype)[None], (B, N_META, D))
    h = jnp.concatenate([meta, x, jnp.zeros((B, L_pad - L, D), x.dtype)], axis=1)
    cos_t, sin_t = _rope_tables(L_pad)

    q, k, v, sga, pm = _mix_in(h, g_mix[0][None], w_in[0].astype(bf), cos_t, sin_t,
                               w_pool_grp[0].astype(bf), pool_scale[0][None], w_pool_br[0].astype(bf))
    attn = _diffattn(lam[0], g_subln[0][None], q, k, v)
    out = _mix_out(h, attn, sga, pm, w_attn_br[0].astype(bf), w_out[0].astype(bf), g_ffn[0][None],
                   w_up[0].astype(bf), conv_w[0], conv_b[0][None], w_down[0].astype(bf), g_final[None])
    return out[:, N_META:N_META + S]
```

```python
import math

import jax
import jax.numpy as jnp
from jax import lax
from jax.experimental import pallas as pl
from jax.experimental.pallas import tpu as pltpu

D_MODEL = 1024
N_META = 16
N_HEADS = 8
HEAD_DIM = 64
V_DIM = 2 * HEAD_DIM
QK_WIDTH = N_HEADS * 2 * HEAD_DIM
ATTN_WIDTH = N_HEADS * V_DIM
POOL_GROUPS = 4
POOL_WINDOWS = (2, 4, 8, 16)
POOL_WIDTH = 512
POOL_GDIM = POOL_WIDTH // POOL_GROUPS
D_FF = 2816
CONV_WIDTH = 3
ROPE_THETA = 10000.0
EPS = 1e-6
LAM_INIT = 0.8 - 0.6 * math.exp(-0.3 * 0)
LOG2E = math.log2(math.e)

COL_Q = 0
COL_K = QK_WIDTH
COL_V = 2 * QK_WIDTH
COL_U = 2 * QK_WIDTH + ATTN_WIDTH
COL_GA = COL_U + POOL_WIDTH
COL_GP = COL_GA + D_MODEL

LANES = 128
MXU_TILE = 256
POOL_HALO = 16
CONV_HALO = 8
VMEM_LIMIT = 56 * 1024 * 1024

ROW_ALIGN = 768
TM_IN = 768
TQ = 768
TK = MXU_TILE
TM_OUT = 384
FF_CHUNK = 256


def _rms_scale(x):
    return lax.rsqrt(jnp.mean(x * x, axis=-1, keepdims=True) + EPS)


def _resident(shape):
    return pl.BlockSpec(shape, lambda *_: (0,) * len(shape), pipeline_mode=pl.Buffered(1))


def _mix_in_kernel(h_ref, g_ref, w_in_ref, cos_ref, sin_ref, wgrp_ref, pscale_ref, wpbr_ref,
                   q_ref, k_ref, v_ref, sga_ref, pm_ref, ubuf_ref):
    t = pl.program_id(1)
    tm = h_ref.shape[0]

    h = h_ref[...]
    hn = ((h * _rms_scale(h)) * g_ref[...]).astype(jnp.bfloat16)

    def proj(col, width):
        return jnp.dot(hn, w_in_ref[:, col:col + width], preferred_element_type=jnp.float32)

    cos = cos_ref[...]
    sin = sin_ref[...]
    lane = lax.broadcasted_iota(jnp.int32, (tm, LANES), 1)
    first_half = (lane % HEAD_DIM) < (HEAD_DIM // 2)

    def rope_store(col, out_ref, scale):
        pj = proj(col, QK_WIDTH)
        for hh in range(N_HEADS):
            xh = pj[:, hh * LANES:(hh + 1) * LANES]
            partner = jnp.where(first_half,
                                pltpu.roll(xh, LANES - HEAD_DIM // 2, axis=1),
                                pltpu.roll(xh, HEAD_DIM // 2, axis=1))
            r = xh * cos + partner * sin
            if scale != 1.0:
                r = r * scale
            out_ref[:, hh * LANES:(hh + 1) * LANES] = r.astype(out_ref.dtype)

    rope_store(COL_Q, q_ref, LOG2E / math.sqrt(HEAD_DIM))
    rope_store(COL_K, k_ref, 1.0)
    v_ref[...] = proj(COL_V, ATTN_WIDTH).astype(v_ref.dtype)
    sga_ref[...] = jax.nn.sigmoid(proj(COL_GA, D_MODEL)).astype(sga_ref.dtype)

    @pl.when(t == 0)
    def _():
        ubuf_ref[0:POOL_HALO, :] = jnp.zeros((POOL_HALO, POOL_WIDTH), jnp.float32)

    ubuf_ref[POOL_HALO:POOL_HALO + tm, :] = proj(COL_U, POOL_WIDTH)
    pos = t * tm + lax.broadcasted_iota(jnp.int32, (tm, 1), 0)
    mixed = []
    for g, w in enumerate(POOL_WINDOWS):
        cols = slice(g * POOL_GDIM, (g + 1) * POOL_GDIM)
        u_g = ubuf_ref[POOL_HALO:POOL_HALO + tm, cols]
        sums = u_g
        for j in range(1, w):
            sums = sums + ubuf_ref[POOL_HALO - j:POOL_HALO - j + tm, cols]
        count = jnp.minimum(pos + 1, w).astype(jnp.float32)
        pooled = (sums / count - u_g).astype(jnp.bfloat16)
        m_g = jnp.dot(pooled, wgrp_ref[g], preferred_element_type=jnp.float32)
        mixed.append((m_g * pscale_ref[:, cols]).astype(jnp.bfloat16))
    pool = jnp.concatenate(mixed, axis=-1)
    pbr = jnp.dot(pool, wpbr_ref[...], preferred_element_type=jnp.float32)
    pm_ref[...] = (jax.nn.sigmoid(proj(COL_GP, D_MODEL)) * pbr).astype(pm_ref.dtype)

    ubuf_ref[0:POOL_HALO, :] = ubuf_ref[tm:tm + POOL_HALO, :]


def _mix_in(h, g_mix, w_in, cos_t, sin_t, w_grp, pool_scale, w_pool_br):
    B, L, D = h.shape
    tm = TM_IN
    assert L % tm == 0
    row = lambda width: pl.BlockSpec((None, tm, width), lambda b, t: (b, t, 0))
    out_sds = jax.ShapeDtypeStruct((B, L, D), jnp.bfloat16)
    return pl.pallas_call(
        _mix_in_kernel,
        grid=(B, L // tm),
        in_specs=[
            row(D),
            _resident((1, D)),
            _resident(w_in.shape),
            pl.BlockSpec((tm, LANES), lambda b, t: (t, 0)),
            pl.BlockSpec((tm, LANES), lambda b, t: (t, 0)),
            _resident(w_grp.shape),
            _resident((1, POOL_WIDTH)),
            _resident(w_pool_br.shape),
        ],
        out_specs=[row(D)] * 5,
        out_shape=[out_sds] * 5,
        scratch_shapes=[pltpu.VMEM((tm + POOL_HALO, POOL_WIDTH), jnp.float32)],
        compiler_params=pltpu.CompilerParams(
            dimension_semantics=("arbitrary", "arbitrary"), vmem_limit_bytes=VMEM_LIMIT),
        name="mix_in",
    )(h, g_mix, w_in, cos_t, sin_t, w_grp, pool_scale, w_pool_br)


def _diffattn_kernel(lam_ref, gsub_ref, q_ref, k_ref, v_ref, o_ref,
                     qq_ref, vt_ref, s_ref, cmax_ref, m_ref, l_ref, acc_ref):
    qi = pl.program_id(2)
    n_kv = vt_ref.shape[0]
    n_qblk = TQ // LANES
    width = 2 * TQ
    diag_chunks = TQ // TK

    @pl.when(qi == 0)
    def _():
        def tr(j, c):
            vt_ref[j] = v_ref[pl.ds(pl.multiple_of(j * TK, TK), TK), :].T
            return c
        lax.fori_loop(0, n_kv, tr, 0)

    qt = q_ref[...].T
    zeros = jnp.zeros((HEAD_DIM, LANES), qt.dtype)
    for blk in range(n_qblk):
        src = slice(blk * LANES, (blk + 1) * LANES)
        c0 = slice(2 * blk * LANES, (2 * blk + 1) * LANES)
        c1 = slice((2 * blk + 1) * LANES, (2 * blk + 2) * LANES)
        qq_ref[0:HEAD_DIM, c0] = qt[0:HEAD_DIM, src]
        qq_ref[HEAD_DIM:V_DIM, c0] = zeros
        qq_ref[0:HEAD_DIM, c1] = zeros
        qq_ref[HEAD_DIM:V_DIM, c1] = qt[HEAD_DIM:V_DIM, src]

    m_ref[...] = jnp.full(m_ref.shape, -jnp.inf, jnp.float32)
    l_ref[...] = jnp.zeros(l_ref.shape, jnp.float32)
    acc_ref[...] = jnp.zeros(acc_ref.shape, jnp.float32)

    def scores(j, slot, diag=None):
        lo = 0 if diag is None else diag * 2 * TK
        kc = k_ref[pl.ds(pl.multiple_of(j * TK, TK), TK), :]
        s = jnp.dot(kc, qq_ref[:, lo:], preferred_element_type=jnp.float32)
        if diag is not None:
            key = diag * TK + lax.broadcasted_iota(jnp.int32, s.shape, 0)
            lane = lo + lax.broadcasted_iota(jnp.int32, s.shape, 1)
            qry = (lane // (2 * LANES)) * LANES + lane % LANES
            s = jnp.where(key <= qry, s, -jnp.inf)
        s_ref[slot, :, lo:] = s
        cmax_ref[slot, :, lo:] = jnp.max(s, axis=0, keepdims=True)

    def consume(j, slot, lo=0):
        m = m_ref[:, lo:]
        m_new = jnp.maximum(m, cmax_ref[slot, :, lo:])
        alpha = jnp.exp2(m - m_new)
        p = jnp.exp2(s_ref[slot, :, lo:] - m_new)
        l_ref[:, lo:] = alpha * l_ref[:, lo:] + jnp.sum(p, axis=0, keepdims=True)
        pv = jnp.dot(vt_ref[j], p.astype(jnp.bfloat16), preferred_element_type=jnp.float32)
        acc_ref[:, lo:] = alpha * acc_ref[:, lo:] + pv
        m_ref[:, lo:] = m_new

    n_slots = diag_chunks
    scores(0, 0)

    def body(i, c):
        base = i * n_slots
        for r in range(n_slots):
            scores(base + r + 1, (r + 1) % n_slots)
            consume(base + r, r)
        return c

    lax.fori_loop(0, qi, body, 0)
    base = qi * n_slots
    scores(base, 0, diag=0)
    for d in range(diag_chunks):
        if d + 1 < diag_chunks:
            scores(base + d + 1, d + 1, diag=d + 1)
        consume(base + d, d, lo=d * 2 * TK)
    l = l_ref[...]
    acc = acc_ref[...]

    lam = lam_ref[...]
    lam_full = (jnp.exp(jnp.sum(lam[0:1, :] * lam[1:2, :], axis=-1, keepdims=True))
                - jnp.exp(jnp.sum(lam[2:3, :] * lam[3:4, :], axis=-1, keepdims=True)) + LAM_INIT)
    gsub = gsub_ref[...] * (1.0 - LAM_INIT)
    for blk in range(n_qblk):
        c0 = slice(2 * blk * LANES, (2 * blk + 1) * LANES)
        c1 = slice((2 * blk + 1) * LANES, (2 * blk + 2) * LANES)
        o = acc[:, c0] / l[:, c0] - lam_full * (acc[:, c1] / l[:, c1])
        o = o * lax.rsqrt(jnp.mean(o * o, axis=0, keepdims=True) + EPS) * gsub
        o_ref[blk * LANES:(blk + 1) * LANES, :] = o.T.astype(o_ref.dtype)


def _diffattn(lam, g_subln, q, k, v):
    B, L, _ = q.shape
    assert TQ % TK == 0 and L % TQ == 0 and TQ % LANES == 0
    kv_spec = pl.BlockSpec((None, L, V_DIM), lambda b, h, i: (b, 0, h))
    return pl.pallas_call(
        _diffattn_kernel,
        grid=(B, N_HEADS, L // TQ),
        in_specs=[
            _resident(lam.shape),
            _resident((V_DIM, 1)),
            pl.BlockSpec((None, TQ, V_DIM), lambda b, h, i: (b, i, h)),
            kv_spec,
            kv_spec,
        ],
        out_specs=pl.BlockSpec((None, TQ, V_DIM), lambda b, h, i: (b, i, h)),
        out_shape=jax.ShapeDtypeStruct((B, L, ATTN_WIDTH), jnp.bfloat16),
        scratch_shapes=[pltpu.VMEM((V_DIM, 2 * TQ), jnp.bfloat16),
                        pltpu.VMEM((L // TK, V_DIM, TK), jnp.bfloat16),
                        pltpu.VMEM((TQ // TK, TK, 2 * TQ), jnp.float32),
                        pltpu.VMEM((TQ // TK, 1, 2 * TQ), jnp.float32),
                        pltpu.VMEM((1, 2 * TQ), jnp.float32),
                        pltpu.VMEM((1, 2 * TQ), jnp.float32),
                        pltpu.VMEM((V_DIM, 2 * TQ), jnp.float32)],
        compiler_params=pltpu.CompilerParams(
            dimension_semantics=("arbitrary", "arbitrary", "arbitrary"), vmem_limit_bytes=VMEM_LIMIT),
        name="diffattn",
    )(lam, g_subln, q, k, v)


def _mix_out_kernel(h_ref, attn_ref, sga_ref, pm_ref, wabr_ref, wout_ref, gffn_ref, wup_ref,
                    convw_ref, convb_ref, wdown_ref, gfin_ref, o_ref, cbuf_ref, halo_ref):
    t = pl.program_id(1)
    tm = h_ref.shape[0]
    ch = FF_CHUNK

    abr = jnp.dot(attn_ref[...], wabr_ref[...], preferred_element_type=jnp.float32)
    merged = sga_ref[...].astype(jnp.float32) * abr + pm_ref[...].astype(jnp.float32)
    h1 = h_ref[...] + jnp.dot(merged.astype(jnp.bfloat16), wout_ref[...],
                              preferred_element_type=jnp.float32)
    hn = ((h1 * _rms_scale(h1)) * gffn_ref[...]).astype(jnp.bfloat16)

    @pl.when(t == 0)
    def _():
        halo_ref[...] = jnp.zeros_like(halo_ref)

    def conv(col):
        up = jnp.dot(hn, wup_ref[:, col:col + ch], preferred_element_type=jnp.float32)
        cbuf_ref[0:CONV_HALO, :] = halo_ref[:, col:col + ch]
        cbuf_ref[CONV_HALO:CONV_HALO + tm, :] = up
        halo_ref[:, col:col + ch] = cbuf_ref[tm:tm + CONV_HALO, :]
        y = convb_ref[:, col:col + ch] + convw_ref[CONV_WIDTH - 1:CONV_WIDTH, col:col + ch] * up
        for j in range(1, CONV_WIDTH):
            wj = convw_ref[CONV_WIDTH - 1 - j:CONV_WIDTH - j, col:col + ch]
            y = y + wj * cbuf_ref[CONV_HALO - j:CONV_HALO - j + tm, :]
        return y

    acc = jnp.zeros((tm, D_MODEL), jnp.float32)
    for c in range(D_FF // ch):
        val = conv(c * ch)
        gate = conv(D_FF + c * ch)
        act = (jax.nn.silu(gate) * val).astype(jnp.bfloat16)
        acc = acc + jnp.dot(act, wdown_ref[c * ch:(c + 1) * ch, :], preferred_element_type=jnp.float32)

    h2 = h1 + acc
    o_ref[...] = ((h2 * _rms_scale(h2)) * gfin_ref[...]).astype(o_ref.dtype)


def _mix_out(h, attn, sga, pm, w_attn_br, w_out, g_ffn, w_up, conv_w, conv_b, w_down, g_final):
    B, L, D = h.shape
    tm = TM_OUT
    assert D_FF % FF_CHUNK == 0 and L % tm == 0
    row = pl.BlockSpec((None, tm, D), lambda b, t: (b, t, 0))
    return pl.pallas_call(
        _mix_out_kernel,
        grid=(B, L // tm),
        in_specs=[
            row, row, row, row,
            _resident(w_attn_br.shape),
            _resident(w_out.shape),
            _resident((1, D)),
            _resident(w_up.shape),
            _resident(conv_w.shape),
            _resident(conv_b.shape),
            _resident(w_down.shape),
            _resident((1, D)),
        ],
        out_specs=row,
        out_shape=jax.ShapeDtypeStruct((B, L, D), jnp.float32),
        scratch_shapes=[pltpu.VMEM((tm + CONV_HALO, FF_CHUNK), jnp.float32),
                        pltpu.VMEM((CONV_HALO, 2 * D_FF), jnp.float32)],
        compiler_params=pltpu.CompilerParams(
            dimension_semantics=("arbitrary", "arbitrary"), vmem_limit_bytes=VMEM_LIMIT),
        name="mix_out",
    )(h, attn, sga, pm, w_attn_br, w_out, g_ffn, w_up, conv_w, conv_b, w_down, g_final)


def _rope_tables(length):
    pos = jnp.arange(length, dtype=jnp.float32)
    inv = 1.0 / (ROPE_THETA ** (jnp.arange(0, HEAD_DIM, 2, dtype=jnp.float32) / HEAD_DIM))
    ang = pos[:, None] * inv[None, :]
    cos, sin = jnp.cos(ang), jnp.sin(ang)
    cos_t = jnp.concatenate([cos, cos, cos, cos], axis=-1)
    sin_t = jnp.concatenate([-sin, sin, -sin, sin], axis=-1)
    return cos_t, sin_t


def kernel(x, meta_tokens, g_mix, w_in, lam, g_subln, w_pool_grp, pool_scale, w_attn_br, w_pool_br,
           w_out, g_ffn, w_up, conv_w, conv_b, w_down, g_final):
    B, S, D = x.shape
    L = N_META + S
    L_pad = ((L + ROW_ALIGN - 1) // ROW_ALIGN) * ROW_ALIGN
    bf = jnp.bfloat16
    meta = jnp.broadcast_to(meta_tokens.astype(x.dtype)[None], (B, N_META, D))
    h = jnp.concatenate([meta, x, jnp.zeros((B, L_pad - L, D), x.dtype)], axis=1)
    cos_t, sin_t = _rope_tables(L_pad)

    q, k, v, sga, pm = _mix_in(h, g_mix[0][None], w_in[0].astype(bf), cos_t, sin_t,
                               w_pool_grp[0].astype(bf), pool_scale[0][None], w_pool_br[0].astype(bf))
    attn = _diffattn(lam[0], g_subln[0][:, None], q, k, v)
    out = _mix_out(h, attn, sga, pm, w_attn_br[0].astype(bf), w_out[0].astype(bf), g_ffn[0][None],
                   w_up[0].astype(bf), conv_w[0], conv_b[0][None], w_down[0].astype(bf), g_final[None])
    return out[:, N_META:N_META + S]
```

```python
import math

import jax
import jax.numpy as jnp
from jax import lax
from jax.experimental import pallas as pl
from jax.experimental.pallas import tpu as pltpu

D_MODEL = 1024
N_META = 16
N_HEADS = 8
HEAD_DIM = 64
V_DIM = 2 * HEAD_DIM
QK_WIDTH = N_HEADS * 2 * HEAD_DIM
ATTN_WIDTH = N_HEADS * V_DIM
POOL_GROUPS = 4
POOL_WINDOWS = (2, 4, 8, 16)
POOL_WIDTH = 512
POOL_GDIM = POOL_WIDTH // POOL_GROUPS
D_FF = 2816
CONV_WIDTH = 3
ROPE_THETA = 10000.0
EPS = 1e-6
LAM_INIT = 0.8 - 0.6 * math.exp(-0.3 * 0)
LOG2E = math.log2(math.e)

COL_Q = 0
COL_K = QK_WIDTH
COL_V = 2 * QK_WIDTH
COL_U = 2 * QK_WIDTH + ATTN_WIDTH
COL_GA = COL_U + POOL_WIDTH
COL_GP = COL_GA + D_MODEL

LANES = 128
MXU_TILE = 256
POOL_HALO = 16
CONV_HALO = 8
VMEM_LIMIT = 56 * 1024 * 1024

ROW_ALIGN = 768
TM_IN = 768
TQ = 768
TK = MXU_TILE
TM_OUT = 704
FF_CHUNK = 256
SUM_ROWS = 16


def _rms_scale(x):
    return lax.rsqrt(jnp.mean(x * x, axis=-1, keepdims=True) + EPS)


def _resident(shape):
    return pl.BlockSpec(shape, lambda *_: (0,) * len(shape), pipeline_mode=pl.Buffered(1))


def _mix_in_kernel(h_ref, g_ref, w_in_ref, cos_ref, sin_ref, wgrp_ref, pscale_ref, wpbr_ref,
                   q_ref, k_ref, v_ref, sga_ref, pm_ref, ubuf_ref):
    t = pl.program_id(1)
    tm = h_ref.shape[0]

    h = h_ref[...]
    hn = ((h * _rms_scale(h)) * g_ref[...]).astype(jnp.bfloat16)

    def proj(col, width):
        return jnp.dot(hn, w_in_ref[:, col:col + width], preferred_element_type=jnp.float32)

    cos = cos_ref[...]
    sin = sin_ref[...]
    lane = lax.broadcasted_iota(jnp.int32, (tm, LANES), 1)
    first_half = (lane % HEAD_DIM) < (HEAD_DIM // 2)

    def rope_store(col, out_ref, scale):
        pj = proj(col, QK_WIDTH)
        for hh in range(N_HEADS):
            xh = pj[:, hh * LANES:(hh + 1) * LANES]
            partner = jnp.where(first_half,
                                pltpu.roll(xh, LANES - HEAD_DIM // 2, axis=1),
                                pltpu.roll(xh, HEAD_DIM // 2, axis=1))
            r = xh * cos + partner * sin
            if scale != 1.0:
                r = r * scale
            out_ref[:, hh * LANES:(hh + 1) * LANES] = r.astype(out_ref.dtype)

    @pl.when(t == 0)
    def _():
        ubuf_ref[0:POOL_HALO, :] = jnp.zeros((POOL_HALO, POOL_WIDTH), jnp.float32)

    ubuf_ref[POOL_HALO:POOL_HALO + tm, :] = proj(COL_U, POOL_WIDTH)

    rope_store(COL_Q, q_ref, LOG2E / math.sqrt(HEAD_DIM))

    pos = t * tm + lax.broadcasted_iota(jnp.int32, (tm, 1), 0)
    mixed = []
    for g, w in enumerate(POOL_WINDOWS):
        cols = slice(g * POOL_GDIM, (g + 1) * POOL_GDIM)
        u_g = ubuf_ref[POOL_HALO:POOL_HALO + tm, cols]
        sums = u_g
        for j in range(1, w):
            sums = sums + ubuf_ref[POOL_HALO - j:POOL_HALO - j + tm, cols]
        count = jnp.minimum(pos + 1, w).astype(jnp.float32)
        pooled = (sums / count - u_g).astype(jnp.bfloat16)
        m_g = jnp.dot(pooled, wgrp_ref[g], preferred_element_type=jnp.float32)
        mixed.append((m_g * pscale_ref[:, cols]).astype(jnp.bfloat16))
    pool = jnp.concatenate(mixed, axis=-1)

    rope_store(COL_K, k_ref, 1.0)
    pbr = jnp.dot(pool, wpbr_ref[...], preferred_element_type=jnp.float32)
    v_ref[...] = proj(COL_V, ATTN_WIDTH).astype(v_ref.dtype)
    pm_ref[...] = (jax.nn.sigmoid(proj(COL_GP, D_MODEL)) * pbr).astype(pm_ref.dtype)
    sga_ref[...] = jax.nn.sigmoid(proj(COL_GA, D_MODEL)).astype(sga_ref.dtype)

    ubuf_ref[0:POOL_HALO, :] = ubuf_ref[tm:tm + POOL_HALO, :]


def _mix_in(h, g_mix, w_in, cos_t, sin_t, w_grp, pool_scale, w_pool_br):
    B, L, D = h.shape
    tm = TM_IN
    assert L % tm == 0
    row = lambda width: pl.BlockSpec((None, tm, width), lambda b, t: (b, t, 0))
    out_sds = jax.ShapeDtypeStruct((B, L, D), jnp.bfloat16)
    return pl.pallas_call(
        _mix_in_kernel,
        grid=(B, L // tm),
        in_specs=[
            row(D),
            _resident((1, D)),
            _resident(w_in.shape),
            pl.BlockSpec((tm, LANES), lambda b, t: (t, 0)),
            pl.BlockSpec((tm, LANES), lambda b, t: (t, 0)),
            _resident(w_grp.shape),
            _resident((1, POOL_WIDTH)),
            _resident(w_pool_br.shape),
        ],
        out_specs=[row(D)] * 5,
        out_shape=[out_sds] * 5,
        scratch_shapes=[pltpu.VMEM((tm + POOL_HALO, POOL_WIDTH), jnp.float32)],
        compiler_params=pltpu.CompilerParams(
            dimension_semantics=("arbitrary", "arbitrary"), vmem_limit_bytes=VMEM_LIMIT),
        name="mix_in",
    )(h, g_mix, w_in, cos_t, sin_t, w_grp, pool_scale, w_pool_br)


def _diffattn_kernel(lam_ref, gsub_ref, q_ref, k_ref, v_ref, o_ref,
                     qq_ref, vt_ref, s_ref, cmax_ref, m_ref, acc_ref):
    qi = pl.program_id(2)
    n_kv = vt_ref.shape[0]
    n_qblk = TQ // LANES
    width = 2 * TQ
    diag_chunks = TQ // TK

    @pl.when(qi == 0)
    def _():
        def tr(j, c):
            vt_ref[j, 0:V_DIM, :] = v_ref[pl.ds(pl.multiple_of(j * TK, TK), TK), :].T
            vt_ref[j, V_DIM:V_DIM + SUM_ROWS, :] = jnp.ones((SUM_ROWS, TK), vt_ref.dtype)
            return c
        lax.fori_loop(0, n_kv, tr, 0)

    qt = q_ref[...].T
    zeros = jnp.zeros((HEAD_DIM, LANES), qt.dtype)
    for blk in range(n_qblk):
        src = slice(blk * LANES, (blk + 1) * LANES)
        c0 = slice(2 * blk * LANES, (2 * blk + 1) * LANES)
        c1 = slice((2 * blk + 1) * LANES, (2 * blk + 2) * LANES)
        qq_ref[0:HEAD_DIM, c0] = qt[0:HEAD_DIM, src]
        qq_ref[HEAD_DIM:V_DIM, c0] = zeros
        qq_ref[0:HEAD_DIM, c1] = zeros
        qq_ref[HEAD_DIM:V_DIM, c1] = qt[HEAD_DIM:V_DIM, src]

    m_ref[...] = jnp.full(m_ref.shape, -jnp.inf, jnp.float32)
    acc_ref[...] = jnp.zeros(acc_ref.shape, jnp.float32)

    def scores(j, slot, diag=None):
        lo = 0 if diag is None else diag * 2 * TK
        kc = k_ref[pl.ds(pl.multiple_of(j * TK, TK), TK), :]
        s = jnp.dot(kc, qq_ref[:, lo:], preferred_element_type=jnp.float32)
        if diag is not None:
            key = diag * TK + lax.broadcasted_iota(jnp.int32, s.shape, 0)
            lane = lo + lax.broadcasted_iota(jnp.int32, s.shape, 1)
            qry = (lane // (2 * LANES)) * LANES + lane % LANES
            s = jnp.where(key <= qry, s, -jnp.inf)
        s_ref[slot, :, lo:] = s
        cmax_ref[slot, :, lo:] = jnp.max(s, axis=0, keepdims=True)

    def consume(j, slot, lo=0):
        m = m_ref[:, lo:]
        m_new = jnp.maximum(m, cmax_ref[slot, :, lo:])
        alpha = jnp.exp2(m - m_new)
        p = jnp.exp2(s_ref[slot, :, lo:] - m_new)
        pv = jnp.dot(vt_ref[j], p.astype(jnp.bfloat16), preferred_element_type=jnp.float32)
        acc_ref[:, lo:] = alpha * acc_ref[:, lo:] + pv
        m_ref[:, lo:] = m_new

    n_slots = diag_chunks
    scores(0, 0)

    def block(base):
        for r in range(n_slots):
            scores(base + r + 1, (r + 1) % n_slots)
            consume(base + r, r)

    def pair(i, c):
        block(2 * i * n_slots)
        block((2 * i + 1) * n_slots)
        return c

    lax.fori_loop(0, qi // 2, pair, 0)

    @pl.when(qi % 2 == 1)
    def _():
        block((qi - 1) * n_slots)

    base = qi * n_slots
    scores(base, 0, diag=0)
    for d in range(diag_chunks):
        if d + 1 < diag_chunks:
            scores(base + d + 1, d + 1, diag=d + 1)
        consume(base + d, d, lo=d * 2 * TK)
    l = acc_ref[V_DIM:V_DIM + 1, :]
    acc = acc_ref[0:V_DIM, :]

    lam = lam_ref[...]
    lam_full = (jnp.exp(jnp.sum(lam[0:1, :] * lam[1:2, :], axis=-1, keepdims=True))
                - jnp.exp(jnp.sum(lam[2:3, :] * lam[3:4, :], axis=-1, keepdims=True)) + LAM_INIT)
    gsub = gsub_ref[...] * (1.0 - LAM_INIT)
    for blk in range(n_qblk):
        c0 = slice(2 * blk * LANES, (2 * blk + 1) * LANES)
        c1 = slice((2 * blk + 1) * LANES, (2 * blk + 2) * LANES)
        o = acc[:, c0] / l[:, c0] - lam_full * (acc[:, c1] / l[:, c1])
        o = o * lax.rsqrt(jnp.mean(o * o, axis=0, keepdims=True) + EPS) * gsub
        o_ref[blk * LANES:(blk + 1) * LANES, :] = o.T.astype(o_ref.dtype)


def _diffattn(lam, g_subln, q, k, v):
    B, L, _ = q.shape
    assert TQ % TK == 0 and L % TQ == 0 and TQ % LANES == 0
    kv_spec = pl.BlockSpec((None, L, V_DIM), lambda b, h, i: (b, 0, h))
    return pl.pallas_call(
        _diffattn_kernel,
        grid=(B, N_HEADS, L // TQ),
        in_specs=[
            _resident(lam.shape),
            _resident((V_DIM, 1)),
            pl.BlockSpec((None, TQ, V_DIM), lambda b, h, i: (b, i, h)),
            kv_spec,
            kv_spec,
        ],
        out_specs=pl.BlockSpec((None, TQ, V_DIM), lambda b, h, i: (b, i, h)),
        out_shape=jax.ShapeDtypeStruct((B, L, ATTN_WIDTH), jnp.bfloat16),
        scratch_shapes=[pltpu.VMEM((V_DIM, 2 * TQ), jnp.bfloat16),
                        pltpu.VMEM((L // TK, V_DIM + SUM_ROWS, TK), jnp.bfloat16),
                        pltpu.VMEM((TQ // TK, TK, 2 * TQ), jnp.float32),
                        pltpu.VMEM((TQ // TK, 1, 2 * TQ), jnp.float32),
                        pltpu.VMEM((1, 2 * TQ), jnp.float32),
                        pltpu.VMEM((V_DIM + SUM_ROWS, 2 * TQ), jnp.float32)],
        compiler_params=pltpu.CompilerParams(
            dimension_semantics=("arbitrary", "arbitrary", "arbitrary"), vmem_limit_bytes=VMEM_LIMIT),
        name="diffattn",
    )(lam, g_subln, q, k, v)


def _mix_out_kernel(h_ref, attn_ref, sga_ref, pm_ref, wabr_ref, wout_ref, gffn_ref, wup_ref,
                    convw_ref, convb_ref, wdown_ref, gfin_ref, o_ref, halo_ref):
    t = pl.program_id(1)
    tm = h_ref.shape[0]
    ch = FF_CHUNK

    abr = jnp.dot(attn_ref[...], wabr_ref[...], preferred_element_type=jnp.float32)
    merged = sga_ref[...].astype(jnp.float32) * abr + pm_ref[...].astype(jnp.float32)
    h1 = h_ref[...] + jnp.dot(merged.astype(jnp.bfloat16), wout_ref[...],
                              preferred_element_type=jnp.float32)
    hn = ((h1 * _rms_scale(h1)) * gffn_ref[...]).astype(jnp.bfloat16)

    @pl.when(t == 0)
    def _():
        halo_ref[...] = jnp.zeros_like(halo_ref)

    def up_proj(c):
        return tuple(jnp.dot(hn, wup_ref[:, col:col + ch], preferred_element_type=jnp.float32)
                     for col in (c * ch, D_FF + c * ch))

    def conv(up, col):
        taps = [convw_ref[CONV_WIDTH - 1 - j:CONV_WIDTH - j, col:col + ch] for j in range(CONV_WIDTH)]
        bias = convb_ref[:, col:col + ch]
        y = bias + taps[0] * up
        for j in range(1, CONV_WIDTH):
            y = y + taps[j] * pltpu.roll(up, j, axis=0)
        ext = jnp.concatenate([halo_ref[:, col:col + ch], up[0:CONV_HALO, :]], axis=0)
        y0 = bias + taps[0] * ext[CONV_HALO:2 * CONV_HALO, :]
        for j in range(1, CONV_WIDTH):
            y0 = y0 + taps[j] * ext[CONV_HALO - j:2 * CONV_HALO - j, :]
        halo_ref[:, col:col + ch] = up[tm - CONV_HALO:tm, :]
        return jnp.concatenate([y0, y[CONV_HALO:, :]], axis=0)

    n_chunks = D_FF // ch
    acc = jnp.zeros((tm, D_MODEL), jnp.float32)
    up = up_proj(0)
    for c in range(n_chunks):
        up_next = up_proj(c + 1) if c + 1 < n_chunks else None
        val = conv(up[0], c * ch)
        gate = conv(up[1], D_FF + c * ch)
        act = (jax.nn.silu(gate) * val).astype(jnp.bfloat16)
        acc = acc + jnp.dot(act, wdown_ref[c * ch:(c + 1) * ch, :], preferred_element_type=jnp.float32)
        up = up_next

    h2 = h1 + acc
    o_ref[...] = ((h2 * _rms_scale(h2)) * gfin_ref[...]).astype(o_ref.dtype)


def _mix_out(h, attn, sga, pm, w_attn_br, w_out, g_ffn, w_up, conv_w, conv_b, w_down, g_final):
    B, L, D = h.shape
    tm = TM_OUT
    assert D_FF % FF_CHUNK == 0 and L % tm == 0
    row = pl.BlockSpec((None, tm, D), lambda b, t: (b, t, 0))
    return pl.pallas_call(
        _mix_out_kernel,
        grid=(B, L // tm),
        in_specs=[
            row, row, row, row,
            _resident(w_attn_br.shape),
            _resident(w_out.shape),
            _resident((1, D)),
            _resident(w_up.shape),
            _resident(conv_w.shape),
            _resident(conv_b.shape),
            _resident(w_down.shape),
            _resident((1, D)),
        ],
        out_specs=row,
        out_shape=jax.ShapeDtypeStruct((B, L, D), jnp.float32),
        scratch_shapes=[pltpu.VMEM((CONV_HALO, 2 * D_FF), jnp.float32)],
        compiler_params=pltpu.CompilerParams(
            dimension_semantics=("arbitrary", "arbitrary"), vmem_limit_bytes=VMEM_LIMIT),
        name="mix_out",
    )(h, attn, sga, pm, w_attn_br, w_out, g_ffn, w_up, conv_w, conv_b, w_down, g_final)


def _rope_tables(length):
    pos = jnp.arange(length, dtype=jnp.float32)
    inv = 1.0 / (ROPE_THETA ** (jnp.arange(0, HEAD_DIM, 2, dtype=jnp.float32) / HEAD_DIM))
    ang = pos[:, None] * inv[None, :]
    cos, sin = jnp.cos(ang), jnp.sin(ang)
    cos_t = jnp.concatenate([cos, cos, cos, cos], axis=-1)
    sin_t = jnp.concatenate([-sin, sin, -sin, sin], axis=-1)
    return cos_t, sin_t


def kernel(x, meta_tokens, g_mix, w_in, lam, g_subln, w_pool_grp, pool_scale, w_attn_br, w_pool_br,
           w_out, g_ffn, w_up, conv_w, conv_b, w_down, g_final):
    B, S, D = x.shape
    L = N_META + S
    L_pad = ((L + ROW_ALIGN - 1) // ROW_ALIGN) * ROW_ALIGN
    bf = jnp.bfloat16
    meta = jnp.broadcast_to(meta_tokens.astype(x.dtype)[None], (B, N_META, D))
    h = jnp.concatenate([meta, x, jnp.zeros((B, L_pad - L, D), x.dtype)], axis=1)
    cos_t, sin_t = _rope_tables(L_pad)

    q, k, v, sga, pm = _mix_in(h, g_mix[0][None], w_in[0].astype(bf), cos_t, sin_t,
                               w_pool_grp[0].astype(bf), pool_scale[0][None], w_pool_br[0].astype(bf))
    attn = _diffattn(lam[0], g_subln[0][:, None], q, k, v)
    out = _mix_out(h, attn, sga, pm, w_attn_br[0].astype(bf), w_out[0].astype(bf), g_ffn[0][None],
                   w_up[0].astype(bf), conv_w[0], conv_b[0][None], w_down[0].astype(bf), g_final[None])
    return out[:, N_META:N_META + S]
```

```python
import functools
import math

import jax
import jax.numpy as jnp
from jax import lax
from jax.experimental import pallas as pl
from jax.experimental.pallas import tpu as pltpu

D_MODEL = 1024
N_META = 16
N_HEADS = 8
HEAD_DIM = 64
V_DIM = 2 * HEAD_DIM
QK_WIDTH = N_HEADS * 2 * HEAD_DIM
ATTN_WIDTH = N_HEADS * V_DIM
POOL_GROUPS = 4
POOL_WINDOWS = (2, 4, 8, 16)
POOL_WIDTH = 512
POOL_GDIM = POOL_WIDTH // POOL_GROUPS
D_FF = 2816
CONV_WIDTH = 3
ROPE_THETA = 10000.0
EPS = 1e-6
LAM_INIT = 0.8 - 0.6 * math.exp(-0.3 * 0)
LOG2E = math.log2(math.e)

COL_Q = 0
COL_K = QK_WIDTH
COL_V = 2 * QK_WIDTH
COL_U = 2 * QK_WIDTH + ATTN_WIDTH
COL_GA = COL_U + POOL_WIDTH
COL_GP = COL_GA + D_MODEL

LANES = 128
MXU_TILE = 256
POOL_HALO = 16
CONV_HALO = 8
VMEM_LIMIT = 56 * 1024 * 1024

ROW_ALIGN = 768
TM_IN = 768
X_BLOCK = 256
TQ = 768
TK = MXU_TILE
SUM_ROWS = 16
TM_OUT = 704
FF_CHUNK = 256
DOWN_GROUP = 6


def _rms_scale(x):
    return lax.rsqrt(jnp.mean(x * x, axis=-1, keepdims=True) + EPS)


def _resident(shape):
    return pl.BlockSpec(shape, lambda *_: (0,) * len(shape), pipeline_mode=pl.Buffered(1))


def _mix_in_kernel(xf_ref, xa_ref, xb_ref, xc_ref, meta_ref, g_ref, w_in_ref, cos_ref, sin_ref,
                   wgrp_ref, pscale_ref, wpbr_ref,
                   h_ref, qq_ref, k_ref, vt_ref, sga_ref, pm_ref, ubuf_ref, *, seq_len):
    t = pl.program_id(1)
    tm = h_ref.shape[0]
    pos = t * tm + lax.broadcasted_iota(jnp.int32, (tm, 1), 0)

    h_ref[0:N_META, :] = jnp.where(t == 0, meta_ref[...], xf_ref[...])
    h_ref[N_META:N_META + X_BLOCK, :] = xa_ref[...]
    h_ref[N_META + X_BLOCK:N_META + 2 * X_BLOCK, :] = xb_ref[...]
    h_ref[N_META + 2 * X_BLOCK:tm, :] = xc_ref[0:tm - N_META - 2 * X_BLOCK, :]
    h = jnp.where(pos < seq_len, h_ref[...], 0.0)
    h_ref[...] = h
    hn = ((h * _rms_scale(h)) * g_ref[...]).astype(jnp.bfloat16)

    def proj(col, width):
        return jnp.dot(hn, w_in_ref[:, col:col + width], preferred_element_type=jnp.float32)

    cos = cos_ref[...]
    sin = sin_ref[...]
    lane = lax.broadcasted_iota(jnp.int32, (tm, LANES), 1)
    first_half = (lane % HEAD_DIM) < (HEAD_DIM // 2)

    def rope(pj, hh):
        xh = pj[:, hh * LANES:(hh + 1) * LANES]
        partner = jnp.where(first_half,
                            pltpu.roll(xh, LANES - HEAD_DIM // 2, axis=1),
                            pltpu.roll(xh, HEAD_DIM // 2, axis=1))
        return xh * cos + partner * sin

    @pl.when(t == 0)
    def _():
        ubuf_ref[0:POOL_HALO, :] = jnp.zeros((POOL_HALO, POOL_WIDTH), jnp.float32)

    ubuf_ref[POOL_HALO:POOL_HALO + tm, :] = proj(COL_U, POOL_WIDTH)

    pj = proj(COL_Q, QK_WIDTH)
    zeros = jnp.zeros((HEAD_DIM, LANES), qq_ref.dtype)
    for hh in range(N_HEADS):
        qt = (rope(pj, hh) * (LOG2E / math.sqrt(HEAD_DIM))).astype(qq_ref.dtype).T
        for blk in range(tm // LANES):
            src = slice(blk * LANES, (blk + 1) * LANES)
            c0 = slice(2 * blk * LANES, (2 * blk + 1) * LANES)
            c1 = slice((2 * blk + 1) * LANES, (2 * blk + 2) * LANES)
            qq_ref[hh, 0:HEAD_DIM, c0] = qt[0:HEAD_DIM, src]
            qq_ref[hh, HEAD_DIM:V_DIM, c0] = zeros
            qq_ref[hh, 0:HEAD_DIM, c1] = zeros
            qq_ref[hh, HEAD_DIM:V_DIM, c1] = qt[HEAD_DIM:V_DIM, src]

    mixed = []
    for g, w in enumerate(POOL_WINDOWS):
        cols = slice(g * POOL_GDIM, (g + 1) * POOL_GDIM)
        u_g = ubuf_ref[POOL_HALO:POOL_HALO + tm, cols]
        sums = u_g
        for j in range(1, w):
            sums = sums + ubuf_ref[POOL_HALO - j:POOL_HALO - j + tm, cols]
        count = jnp.minimum(pos + 1, w).astype(jnp.float32)
        pooled = (sums / count - u_g).astype(jnp.bfloat16)
        m_g = jnp.dot(pooled, wgrp_ref[g], preferred_element_type=jnp.float32)
        mixed.append((m_g * pscale_ref[:, cols]).astype(jnp.bfloat16))
    pool = jnp.concatenate(mixed, axis=-1)

    pj = proj(COL_K, QK_WIDTH)
    for hh in range(N_HEADS):
        k_ref[:, hh * LANES:(hh + 1) * LANES] = rope(pj, hh).astype(k_ref.dtype)
    pbr = jnp.dot(pool, wpbr_ref[...], preferred_element_type=jnp.float32)

    v = proj(COL_V, ATTN_WIDTH).astype(vt_ref.dtype)
    ones = jnp.ones((SUM_ROWS, TK), vt_ref.dtype)
    for hh in range(N_HEADS):
        for c in range(tm // TK):
            vt_ref[hh, c, 0:V_DIM, :] = v[c * TK:(c + 1) * TK, hh * LANES:(hh + 1) * LANES].T
            vt_ref[hh, c, V_DIM:V_DIM + SUM_ROWS, :] = ones
    pm_ref[...] = (jax.nn.sigmoid(proj(COL_GP, D_MODEL)) * pbr).astype(pm_ref.dtype)
    sga_ref[...] = jax.nn.sigmoid(proj(COL_GA, D_MODEL)).astype(sga_ref.dtype)

    ubuf_ref[0:POOL_HALO, :] = ubuf_ref[tm:tm + POOL_HALO, :]


def _mix_in(x, meta, L, g_mix, w_in, cos_t, sin_t, w_grp, pool_scale, w_pool_br):
    B, S, D = x.shape
    tm = TM_IN
    n_xblk = S // X_BLOCK
    assert L % tm == 0 and S % X_BLOCK == 0 and tm == 3 * X_BLOCK and X_BLOCK % N_META == 0
    assert tm % TK == 0 and tm % LANES == 0
    row = lambda width: pl.BlockSpec((None, tm, width), lambda b, t: (b, t, 0))
    x_head = pl.BlockSpec((None, N_META, D), lambda b, t: (b, jnp.maximum(t * (tm // N_META) - 1, 0), 0))
    x_body = lambda i: pl.BlockSpec((None, X_BLOCK, D),
                                    lambda b, t: (b, jnp.minimum(t * (tm // X_BLOCK) + i, n_xblk - 1), 0))
    bf = jnp.bfloat16
    return pl.pallas_call(
        functools.partial(_mix_in_kernel, seq_len=N_META + S),
        grid=(B, L // tm),
        in_specs=[
            x_head, x_body(0), x_body(1), x_body(2),
            _resident((N_META, D)),
            _resident((1, D)),
            _resident(w_in.shape),
            pl.BlockSpec((tm, LANES), lambda b, t: (t, 0)),
            pl.BlockSpec((tm, LANES), lambda b, t: (t, 0)),
            _resident(w_grp.shape),
            _resident((1, POOL_WIDTH)),
            _resident(w_pool_br.shape),
        ],
        out_specs=[
            row(D),
            pl.BlockSpec((None, N_HEADS, V_DIM, 2 * tm), lambda b, t: (b, 0, 0, t)),
            row(QK_WIDTH),
            pl.BlockSpec((None, N_HEADS, tm // TK, V_DIM + SUM_ROWS, TK), lambda b, t: (b, 0, t, 0, 0)),
            row(D),
            row(D),
        ],
        out_shape=[
            jax.ShapeDtypeStruct((B, L, D), jnp.float32),
            jax.ShapeDtypeStruct((B, N_HEADS, V_DIM, 2 * L), bf),
            jax.ShapeDtypeStruct((B, L, QK_WIDTH), bf),
            jax.ShapeDtypeStruct((B, N_HEADS, L // TK, V_DIM + SUM_ROWS, TK), bf),
            jax.ShapeDtypeStruct((B, L, D), bf),
            jax.ShapeDtypeStruct((B, L, D), bf),
        ],
        scratch_shapes=[pltpu.VMEM((tm + POOL_HALO, POOL_WIDTH), jnp.float32)],
        compiler_params=pltpu.CompilerParams(
            dimension_semantics=("arbitrary", "arbitrary"), vmem_limit_bytes=VMEM_LIMIT),
        name="mix_in",
    )(x, x, x, x, meta, g_mix, w_in, cos_t, sin_t, w_grp, pool_scale, w_pool_br)


def _diffattn_kernel(lam_ref, gsub_ref, qq_ref, k_ref, vt_ref, o_ref, s_ref, cmax_ref, m_ref, acc_ref):
    qi = pl.program_id(2)
    n_qblk = TQ // LANES
    diag_chunks = TQ // TK

    m_ref[...] = jnp.full(m_ref.shape, -jnp.inf, jnp.float32)
    acc_ref[...] = jnp.zeros(acc_ref.shape, jnp.float32)

    def scores(j, slot, diag=None):
        lo = 0 if diag is None else diag * 2 * TK
        kc = k_ref[pl.ds(pl.multiple_of(j * TK, TK), TK), :]
        s = jnp.dot(kc, qq_ref[:, lo:], preferred_element_type=jnp.float32)
        if diag is not None:
            key = diag * TK + lax.broadcasted_iota(jnp.int32, s.shape, 0)
            lane = lo + lax.broadcasted_iota(jnp.int32, s.shape, 1)
            qry = (lane // (2 * LANES)) * LANES + lane % LANES
            s = jnp.where(key <= qry, s, -jnp.inf)
        s_ref[slot, :, lo:] = s
        cmax_ref[slot, :, lo:] = jnp.max(s, axis=0, keepdims=True)

    def consume(j, slot, lo=0):
        m = m_ref[:, lo:]
        m_new = jnp.maximum(m, cmax_ref[slot, :, lo:])
        alpha = jnp.exp2(m - m_new)
        p = jnp.exp2(s_ref[slot, :, lo:] - m_new)
        pv = jnp.dot(vt_ref[j], p.astype(jnp.bfloat16), preferred_element_type=jnp.float32)
        acc_ref[:, lo:] = alpha * acc_ref[:, lo:] + pv
        m_ref[:, lo:] = m_new

    n_slots = diag_chunks
    scores(0, 0)

    def block(base):
        for r in range(n_slots):
            scores(base + r + 1, (r + 1) % n_slots)
            consume(base + r, r)

    def pair(i, c):
        block(2 * i * n_slots)
        block((2 * i + 1) * n_slots)
        return c

    lax.fori_loop(0, qi // 2, pair, 0)

    @pl.when(qi % 2 == 1)
    def _():
        block((qi - 1) * n_slots)

    base = qi * n_slots
    scores(base, 0, diag=0)
    for d in range(diag_chunks):
        if d + 1 < diag_chunks:
            scores(base + d + 1, d + 1, diag=d + 1)
        consume(base + d, d, lo=d * 2 * TK)
    l = acc_ref[V_DIM:V_DIM + 1, :]
    acc = acc_ref[0:V_DIM, :]

    lam = lam_ref[...]
    lam_full = (jnp.exp(jnp.sum(lam[0:1, :] * lam[1:2, :], axis=-1, keepdims=True))
                - jnp.exp(jnp.sum(lam[2:3, :] * lam[3:4, :], axis=-1, keepdims=True)) + LAM_INIT)
    gsub = gsub_ref[...] * (1.0 - LAM_INIT)
    for blk in range(n_qblk):
        c0 = slice(2 * blk * LANES, (2 * blk + 1) * LANES)
        c1 = slice((2 * blk + 1) * LANES, (2 * blk + 2) * LANES)
        o = acc[:, c0] / l[:, c0] - lam_full * (acc[:, c1] / l[:, c1])
        o = o * lax.rsqrt(jnp.mean(o * o, axis=0, keepdims=True) + EPS) * gsub
        o_ref[blk * LANES:(blk + 1) * LANES, :] = o.T.astype(o_ref.dtype)


def _diffattn(lam, g_subln, qq, k, vt):
    B, L, _ = k.shape
    assert TQ % TK == 0 and L % TQ == 0 and TQ % LANES == 0
    return pl.pallas_call(
        _diffattn_kernel,
        grid=(B, N_HEADS, L // TQ),
        in_specs=[
            _resident(lam.shape),
            _resident((V_DIM, 1)),
            pl.BlockSpec((None, None, V_DIM, 2 * TQ), lambda b, h, i: (b, h, 0, i)),
            pl.BlockSpec((None, L, V_DIM), lambda b, h, i: (b, 0, h)),
            pl.BlockSpec((None, None, L // TK, V_DIM + SUM_ROWS, TK), lambda b, h, i: (b, h, 0, 0, 0)),
        ],
        out_specs=pl.BlockSpec((None, TQ, V_DIM), lambda b, h, i: (b, i, h)),
        out_shape=jax.ShapeDtypeStruct((B, L, ATTN_WIDTH), jnp.bfloat16),
        scratch_shapes=[pltpu.VMEM((TQ // TK, TK, 2 * TQ), jnp.float32),
                        pltpu.VMEM((TQ // TK, 1, 2 * TQ), jnp.float32),
                        pltpu.VMEM((1, 2 * TQ), jnp.float32),
                        pltpu.VMEM((V_DIM + SUM_ROWS, 2 * TQ), jnp.float32)],
        compiler_params=pltpu.CompilerParams(
            dimension_semantics=("arbitrary", "arbitrary", "arbitrary"), vmem_limit_bytes=VMEM_LIMIT),
        name="diffattn",
    )(lam, g_subln, qq, k, vt)


def _mix_out_kernel(h_ref, attn_ref, sga_ref, pm_ref, wabr_ref, wout_ref, gffn_ref, wup_ref,
                    convw_ref, convb_ref, wdown_ref, gfin_ref, o_ref, halo_ref):
    t = pl.program_id(1)
    tm = h_ref.shape[0]
    ch = FF_CHUNK

    abr = jnp.dot(attn_ref[...], wabr_ref[...], preferred_element_type=jnp.float32)
    merged = sga_ref[...].astype(jnp.float32) * abr + pm_ref[...].astype(jnp.float32)
    h1 = h_ref[...] + jnp.dot(merged.astype(jnp.bfloat16), wout_ref[...],
                              preferred_element_type=jnp.float32)
    hn = ((h1 * _rms_scale(h1)) * gffn_ref[...]).astype(jnp.bfloat16)

    @pl.when(t == 0)
    def _():
        halo_ref[...] = jnp.zeros_like(halo_ref)

    def up_proj(c):
        return tuple(jnp.dot(hn, wup_ref[:, col:col + ch], preferred_element_type=jnp.float32)
                     for col in (c * ch, D_FF + c * ch))

    def conv(up, col):
        taps = [convw_ref[CONV_WIDTH - 1 - j:CONV_WIDTH - j, col:col + ch] for j in range(CONV_WIDTH)]
        bias = convb_ref[:, col:col + ch]
        y = bias + taps[0] * up
        for j in range(1, CONV_WIDTH):
            y = y + taps[j] * pltpu.roll(up, j, axis=0)
        ext = jnp.concatenate([halo_ref[:, col:col + ch], up[0:CONV_HALO, :]], axis=0)
        y0 = bias + taps[0] * ext[CONV_HALO:2 * CONV_HALO, :]
        for j in range(1, CONV_WIDTH):
            y0 = y0 + taps[j] * ext[CONV_HALO - j:2 * CONV_HALO - j, :]
        halo_ref[:, col:col + ch] = up[tm - CONV_HALO:tm, :]
        return jnp.concatenate([y0, y[CONV_HALO:, :]], axis=0)

    n_chunks = D_FF // ch
    acc = h1
    up = up_proj(0)
    acts = []
    for c in range(n_chunks):
        up_next = up_proj(c + 1) if c + 1 < n_chunks else None
        val = conv(up[0], c * ch)
        gate = conv(up[1], D_FF + c * ch)
        acts.append((jax.nn.silu(gate) * val).astype(jnp.bfloat16))
        if len(acts) == DOWN_GROUP or c + 1 == n_chunks:
            lo = (c + 1 - len(acts)) * ch
            act = jnp.concatenate(acts, axis=-1) if len(acts) > 1 else acts[0]
            acc = acc + jnp.dot(act, wdown_ref[lo:(c + 1) * ch, :], preferred_element_type=jnp.float32)
            acts = []
        up = up_next

    h2 = acc
    o_ref[...] = ((h2 * _rms_scale(h2)) * gfin_ref[...]).astype(o_ref.dtype)


def _mix_out(h, attn, sga, pm, w_attn_br, w_out, g_ffn, w_up, conv_w, conv_b, w_down, g_final):
    B, L, D = h.shape
    tm = TM_OUT
    assert D_FF % FF_CHUNK == 0 and L % tm == 0
    row = pl.BlockSpec((None, tm, D), lambda b, t: (b, t, 0))
    return pl.pallas_call(
        _mix_out_kernel,
        grid=(B, L // tm),
        in_specs=[
            row, row, row, row,
            _resident(w_attn_br.shape),
            _resident(w_out.shape),
            _resident((1, D)),
            _resident(w_up.shape),
            _resident(conv_w.shape),
            _resident(conv_b.shape),
            _resident(w_down.shape),
            _resident((1, D)),
        ],
        out_specs=row,
        out_shape=jax.ShapeDtypeStruct((B, L, D), jnp.float32),
        scratch_shapes=[pltpu.VMEM((CONV_HALO, 2 * D_FF), jnp.float32)],
        compiler_params=pltpu.CompilerParams(
            dimension_semantics=("arbitrary", "arbitrary"), vmem_limit_bytes=VMEM_LIMIT),
        name="mix_out",
    )(h, attn, sga, pm, w_attn_br, w_out, g_ffn, w_up, conv_w, conv_b, w_down, g_final)


def _rope_tables(length):
    pos = jnp.arange(length, dtype=jnp.float32)
    inv = 1.0 / (ROPE_THETA ** (jnp.arange(0, HEAD_DIM, 2, dtype=jnp.float32) / HEAD_DIM))
    ang = pos[:, None] * inv[None, :]
    cos, sin = jnp.cos(ang), jnp.sin(ang)
    cos_t = jnp.concatenate([cos, cos, cos, cos], axis=-1)
    sin_t = jnp.concatenate([-sin, sin, -sin, sin], axis=-1)
    return cos_t, sin_t


def kernel(x, meta_tokens, g_mix, w_in, lam, g_subln, w_pool_grp, pool_scale, w_attn_br, w_pool_br,
           w_out, g_ffn, w_up, conv_w, conv_b, w_down, g_final):
    B, S, D = x.shape
    L = N_META + S
    L_pad = ((L + ROW_ALIGN - 1) // ROW_ALIGN) * ROW_ALIGN
    bf = jnp.bfloat16
    cos_t, sin_t = _rope_tables(L_pad)

    h, qq, k, vt, sga, pm = _mix_in(x, meta_tokens.astype(x.dtype), L_pad, g_mix[0][None], w_in[0].astype(bf),
                                    cos_t, sin_t, w_pool_grp[0].astype(bf), pool_scale[0][None],
                                    w_pool_br[0].astype(bf))
    attn = _diffattn(lam[0], g_subln[0][:, None], qq, k, vt)
    out = _mix_out(h, attn, sga, pm, w_attn_br[0].astype(bf), w_out[0].astype(bf), g_ffn[0][None],
                   w_up[0].astype(bf), conv_w[0], conv_b[0][None], w_down[0].astype(bf), g_final[None])
    return out[:, N_META:N_META + S]
```

```python
import functools
import math

import jax
import jax.numpy as jnp
import numpy as np
from jax import lax
from jax.experimental import pallas as pl
from jax.experimental.pallas import tpu as pltpu

D_MODEL = 1024
N_META = 16
N_HEADS = 8
HEAD_DIM = 64
V_DIM = 2 * HEAD_DIM
QK_WIDTH = N_HEADS * 2 * HEAD_DIM
ATTN_WIDTH = N_HEADS * V_DIM
POOL_GROUPS = 4
POOL_WINDOWS = (2, 4, 8, 16)
POOL_WIDTH = 512
POOL_GDIM = POOL_WIDTH // POOL_GROUPS
D_FF = 2816
CONV_WIDTH = 3
ROPE_THETA = 10000.0
EPS = 1e-6
LAM_INIT = 0.8 - 0.6 * math.exp(-0.3 * 0)
LOG2E = math.log2(math.e)

COL_Q = 0
COL_K = QK_WIDTH
COL_V = 2 * QK_WIDTH
COL_U = 2 * QK_WIDTH + ATTN_WIDTH
COL_GA = COL_U + POOL_WIDTH
COL_GP = COL_GA + D_MODEL

LANES = 128
MXU_TILE = 256
POOL_HALO = 16
CONV_HALO = 8
VMEM_LIMIT = 56 * 1024 * 1024

ROW_ALIGN = 768
TM_IN = 768
X_BLOCK = 256
TQ = 768
TK = MXU_TILE
SUM_ROWS = 16
TM_OUT = 704
FF_CHUNK = 256
DOWN_GROUP = 6


def _rms_scale(x):
    return lax.rsqrt(jnp.mean(x * x, axis=-1, keepdims=True) + EPS)


def _resident(shape):
    return pl.BlockSpec(shape, lambda *_: (0,) * len(shape), pipeline_mode=pl.Buffered(1))


def _mix_in_kernel(xf_ref, xa_ref, xb_ref, xc_ref, meta_ref, g_ref, w_in_ref, cos_ref, sin_ref,
                   wgrp_ref, pscale_ref, wpbr_ref,
                   h_ref, qq_ref, k_ref, vt_ref, sga_ref, pm_ref, ubuf_ref, *, seq_len):
    t = pl.program_id(1)
    tm = h_ref.shape[0]
    pos = t * tm + lax.broadcasted_iota(jnp.int32, (tm, 1), 0)

    h_ref[0:N_META, :] = jnp.where(t == 0, meta_ref[...], xf_ref[...])
    h_ref[N_META:N_META + X_BLOCK, :] = xa_ref[...]
    h_ref[N_META + X_BLOCK:N_META + 2 * X_BLOCK, :] = xb_ref[...]
    h_ref[N_META + 2 * X_BLOCK:tm, :] = xc_ref[0:tm - N_META - 2 * X_BLOCK, :]
    h = jnp.where(pos < seq_len, h_ref[...], 0.0)
    h_ref[...] = h
    hn = ((h * _rms_scale(h)) * g_ref[...]).astype(jnp.bfloat16)

    def proj(col, width):
        return jnp.dot(hn, w_in_ref[:, col:col + width], preferred_element_type=jnp.float32)

    cos = cos_ref[...]
    sin = sin_ref[...]
    lane = lax.broadcasted_iota(jnp.int32, (tm, LANES), 1)
    first_half = (lane % HEAD_DIM) < (HEAD_DIM // 2)

    def rope(pj, hh):
        xh = pj[:, hh * LANES:(hh + 1) * LANES]
        partner = jnp.where(first_half,
                            pltpu.roll(xh, LANES - HEAD_DIM // 2, axis=1),
                            pltpu.roll(xh, HEAD_DIM // 2, axis=1))
        return xh * cos + partner * sin

    @pl.when(t == 0)
    def _():
        ubuf_ref[0:POOL_HALO, :] = jnp.zeros((POOL_HALO, POOL_WIDTH), jnp.float32)

    ubuf_ref[POOL_HALO:POOL_HALO + tm, :] = proj(COL_U, POOL_WIDTH)

    pj = proj(COL_Q, QK_WIDTH)
    zeros = jnp.zeros((HEAD_DIM, LANES), qq_ref.dtype)
    for hh in range(N_HEADS):
        qt = (rope(pj, hh) * (LOG2E / math.sqrt(HEAD_DIM))).astype(qq_ref.dtype).T
        for blk in range(tm // LANES):
            src = slice(blk * LANES, (blk + 1) * LANES)
            c0 = slice(2 * blk * LANES, (2 * blk + 1) * LANES)
            c1 = slice((2 * blk + 1) * LANES, (2 * blk + 2) * LANES)
            qq_ref[hh, 0:HEAD_DIM, c0] = qt[0:HEAD_DIM, src]
            qq_ref[hh, HEAD_DIM:V_DIM, c0] = zeros
            qq_ref[hh, 0:HEAD_DIM, c1] = zeros
            qq_ref[hh, HEAD_DIM:V_DIM, c1] = qt[HEAD_DIM:V_DIM, src]

    mixed = []
    for g, w in enumerate(POOL_WINDOWS):
        cols = slice(g * POOL_GDIM, (g + 1) * POOL_GDIM)
        u_g = ubuf_ref[POOL_HALO:POOL_HALO + tm, cols]
        sums = u_g
        for j in range(1, w):
            sums = sums + ubuf_ref[POOL_HALO - j:POOL_HALO - j + tm, cols]
        count = jnp.minimum(pos + 1, w).astype(jnp.float32)
        pooled = (sums / count - u_g).astype(jnp.bfloat16)
        m_g = jnp.dot(pooled, wgrp_ref[g], preferred_element_type=jnp.float32)
        mixed.append((m_g * pscale_ref[:, cols]).astype(jnp.bfloat16))
    pool = jnp.concatenate(mixed, axis=-1)

    pj = proj(COL_K, QK_WIDTH)
    for hh in range(N_HEADS):
        k_ref[:, hh * LANES:(hh + 1) * LANES] = rope(pj, hh).astype(k_ref.dtype)
    pbr = jnp.dot(pool, wpbr_ref[...], preferred_element_type=jnp.float32)

    v = proj(COL_V, ATTN_WIDTH).astype(vt_ref.dtype)
    ones = jnp.ones((SUM_ROWS, TK), vt_ref.dtype)
    for hh in range(N_HEADS):
        for c in range(tm // TK):
            vt_ref[hh, c, 0:V_DIM, :] = v[c * TK:(c + 1) * TK, hh * LANES:(hh + 1) * LANES].T
            vt_ref[hh, c, V_DIM:V_DIM + SUM_ROWS, :] = ones
    pm_ref[...] = (jax.nn.sigmoid(proj(COL_GP, D_MODEL)) * pbr).astype(pm_ref.dtype)
    sga_ref[...] = jax.nn.sigmoid(proj(COL_GA, D_MODEL)).astype(sga_ref.dtype)

    ubuf_ref[0:POOL_HALO, :] = ubuf_ref[tm:tm + POOL_HALO, :]


def _mix_in(x, meta, L, g_mix, w_in, cos_t, sin_t, w_grp, pool_scale, w_pool_br):
    B, S, D = x.shape
    tm = TM_IN
    n_xblk = S // X_BLOCK
    assert L % tm == 0 and S % X_BLOCK == 0 and tm == 3 * X_BLOCK and X_BLOCK % N_META == 0
    assert tm % TK == 0 and tm % LANES == 0
    row = lambda width: pl.BlockSpec((None, tm, width), lambda b, t: (b, t, 0))
    x_head = pl.BlockSpec((None, N_META, D), lambda b, t: (b, jnp.maximum(t * (tm // N_META) - 1, 0), 0))
    x_body = lambda i: pl.BlockSpec((None, X_BLOCK, D),
                                    lambda b, t: (b, jnp.minimum(t * (tm // X_BLOCK) + i, n_xblk - 1), 0))
    bf = jnp.bfloat16
    return pl.pallas_call(
        functools.partial(_mix_in_kernel, seq_len=N_META + S),
        grid=(B, L // tm),
        in_specs=[
            x_head, x_body(0), x_body(1), x_body(2),
            _resident((N_META, D)),
            _resident((1, D)),
            _resident(w_in.shape),
            pl.BlockSpec((tm, LANES), lambda b, t: (t, 0)),
            pl.BlockSpec((tm, LANES), lambda b, t: (t, 0)),
            _resident(w_grp.shape),
            _resident((1, POOL_WIDTH)),
            _resident(w_pool_br.shape),
        ],
        out_specs=[
            row(D),
            pl.BlockSpec((None, N_HEADS, V_DIM, 2 * tm), lambda b, t: (b, 0, 0, t)),
            row(QK_WIDTH),
            pl.BlockSpec((None, N_HEADS, tm // TK, V_DIM + SUM_ROWS, TK), lambda b, t: (b, 0, t, 0, 0)),
            row(D),
            row(D),
        ],
        out_shape=[
            jax.ShapeDtypeStruct((B, L, D), jnp.float32),
            jax.ShapeDtypeStruct((B, N_HEADS, V_DIM, 2 * L), bf),
            jax.ShapeDtypeStruct((B, L, QK_WIDTH), bf),
            jax.ShapeDtypeStruct((B, N_HEADS, L // TK, V_DIM + SUM_ROWS, TK), bf),
            jax.ShapeDtypeStruct((B, L, D), bf),
            jax.ShapeDtypeStruct((B, L, D), bf),
        ],
        scratch_shapes=[pltpu.VMEM((tm + POOL_HALO, POOL_WIDTH), jnp.float32)],
        compiler_params=pltpu.CompilerParams(
            dimension_semantics=("arbitrary", "arbitrary"), vmem_limit_bytes=VMEM_LIMIT),
        name="mix_in",
    )(x, x, x, x, meta, g_mix, w_in, cos_t, sin_t, w_grp, pool_scale, w_pool_br)


def _diffattn_kernel(lam_ref, gsub_ref, qq_ref, k_ref, vt_ref, o_ref, s_ref, cmax_ref, m_ref, acc_ref):
    qi = pl.program_id(2)
    n_qblk = TQ // LANES
    diag_chunks = TQ // TK

    m_ref[...] = jnp.full(m_ref.shape, -jnp.inf, jnp.float32)
    acc_ref[...] = jnp.zeros(acc_ref.shape, jnp.float32)

    def scores(j, slot, diag=None):
        lo = 0 if diag is None else diag * 2 * TK
        kc = k_ref[pl.ds(pl.multiple_of(j * TK, TK), TK), :]
        s = jnp.dot(kc, qq_ref[:, lo:], preferred_element_type=jnp.float32)
        if diag is not None:
            key = diag * TK + lax.broadcasted_iota(jnp.int32, s.shape, 0)
            lane = lo + lax.broadcasted_iota(jnp.int32, s.shape, 1)
            qry = (lane // (2 * LANES)) * LANES + lane % LANES
            s = jnp.where(key <= qry, s, -jnp.inf)
        s_ref[slot, :, lo:] = s
        cmax_ref[slot, :, lo:] = jnp.max(s, axis=0, keepdims=True)

    def consume(j, slot, lo=0):
        m = m_ref[:, lo:]
        m_new = jnp.maximum(m, cmax_ref[slot, :, lo:])
        alpha = jnp.exp2(m - m_new)
        p = jnp.exp2(s_ref[slot, :, lo:] - m_new)
        pv = jnp.dot(vt_ref[j], p.astype(jnp.bfloat16), preferred_element_type=jnp.float32)
        acc_ref[:, lo:] = alpha * acc_ref[:, lo:] + pv
        m_ref[:, lo:] = m_new

    n_slots = diag_chunks
    scores(0, 0)

    def block(base):
        for r in range(n_slots):
            scores(base + r + 1, (r + 1) % n_slots)
            consume(base + r, r)

    def pair(i, c):
        block(2 * i * n_slots)
        block((2 * i + 1) * n_slots)
        return c

    lax.fori_loop(0, qi // 2, pair, 0)

    @pl.when(qi % 2 == 1)
    def _():
        block((qi - 1) * n_slots)

    base = qi * n_slots
    scores(base, 0, diag=0)
    for d in range(diag_chunks):
        if d + 1 < diag_chunks:
            scores(base + d + 1, d + 1, diag=d + 1)
        consume(base + d, d, lo=d * 2 * TK)
    l = acc_ref[V_DIM:V_DIM + 1, :]
    acc = acc_ref[0:V_DIM, :]

    lam = lam_ref[...]
    lam_full = (jnp.exp(jnp.sum(lam[0:1, :] * lam[1:2, :], axis=-1, keepdims=True))
                - jnp.exp(jnp.sum(lam[2:3, :] * lam[3:4, :], axis=-1, keepdims=True)) + LAM_INIT)
    gsub = gsub_ref[...] * (1.0 - LAM_INIT)
    for blk in range(n_qblk):
        c0 = slice(2 * blk * LANES, (2 * blk + 1) * LANES)
        c1 = slice((2 * blk + 1) * LANES, (2 * blk + 2) * LANES)
        o = acc[:, c0] / l[:, c0] - lam_full * (acc[:, c1] / l[:, c1])
        o = o * lax.rsqrt(jnp.mean(o * o, axis=0, keepdims=True) + EPS) * gsub
        o_ref[blk * LANES:(blk + 1) * LANES, :] = o.T.astype(o_ref.dtype)


def _diffattn(lam, g_subln, qq, k, vt):
    B, L, _ = k.shape
    assert TQ % TK == 0 and L % TQ == 0 and TQ % LANES == 0
    return pl.pallas_call(
        _diffattn_kernel,
        grid=(B, N_HEADS, L // TQ),
        in_specs=[
            _resident(lam.shape),
            _resident((V_DIM, 1)),
            pl.BlockSpec((None, None, V_DIM, 2 * TQ), lambda b, h, i: (b, h, 0, i)),
            pl.BlockSpec((None, L, V_DIM), lambda b, h, i: (b, 0, h)),
            pl.BlockSpec((None, None, L // TK, V_DIM + SUM_ROWS, TK), lambda b, h, i: (b, h, 0, 0, 0)),
        ],
        out_specs=pl.BlockSpec((None, TQ, V_DIM), lambda b, h, i: (b, i, h)),
        out_shape=jax.ShapeDtypeStruct((B, L, ATTN_WIDTH), jnp.bfloat16),
        scratch_shapes=[pltpu.VMEM((TQ // TK, TK, 2 * TQ), jnp.float32),
                        pltpu.VMEM((TQ // TK, 1, 2 * TQ), jnp.float32),
                        pltpu.VMEM((1, 2 * TQ), jnp.float32),
                        pltpu.VMEM((V_DIM + SUM_ROWS, 2 * TQ), jnp.float32)],
        compiler_params=pltpu.CompilerParams(
            dimension_semantics=("arbitrary", "arbitrary", "arbitrary"), vmem_limit_bytes=VMEM_LIMIT),
        name="diffattn",
    )(lam, g_subln, qq, k, vt)


def _mix_out_kernel(h_ref, attn_ref, sga_ref, pm_ref, wabr_ref, wout_ref, gffn_ref, wup_ref,
                    convw_ref, convb_ref, wdown_ref, gfin_ref, o_ref, halo_ref, prev_ref):
    t = pl.program_id(1)
    n_tiles = pl.num_programs(1) - 1
    tm = h_ref.shape[0]
    ch = FF_CHUNK

    def emit(first_rows_of_next):
        o_ref[0:tm - N_META, :] = prev_ref[N_META:tm, :]
        o_ref[tm - N_META:tm, :] = first_rows_of_next

    @pl.when(t == n_tiles)
    def _():
        emit(jnp.zeros((N_META, D_MODEL), o_ref.dtype))

    @pl.when(t < n_tiles)
    def _():
        abr = jnp.dot(attn_ref[...], wabr_ref[...], preferred_element_type=jnp.float32)
        merged = sga_ref[...].astype(jnp.float32) * abr + pm_ref[...].astype(jnp.float32)
        h1 = h_ref[...] + jnp.dot(merged.astype(jnp.bfloat16), wout_ref[...],
                                  preferred_element_type=jnp.float32)
        hn = ((h1 * _rms_scale(h1)) * gffn_ref[...]).astype(jnp.bfloat16)

        @pl.when(t == 0)
        def _():
            halo_ref[...] = jnp.zeros_like(halo_ref)

        def up_proj(c):
            return tuple(jnp.dot(hn, wup_ref[:, col:col + ch], preferred_element_type=jnp.float32)
                         for col in (c * ch, D_FF + c * ch))

        def conv(up, col):
            taps = [convw_ref[CONV_WIDTH - 1 - j:CONV_WIDTH - j, col:col + ch] for j in range(CONV_WIDTH)]
            bias = convb_ref[:, col:col + ch]
            y = bias + taps[0] * up
            for j in range(1, CONV_WIDTH):
                y = y + taps[j] * pltpu.roll(up, j, axis=0)
            ext = jnp.concatenate([halo_ref[:, col:col + ch], up[0:CONV_HALO, :]], axis=0)
            y0 = bias + taps[0] * ext[CONV_HALO:2 * CONV_HALO, :]
            for j in range(1, CONV_WIDTH):
                y0 = y0 + taps[j] * ext[CONV_HALO - j:2 * CONV_HALO - j, :]
            halo_ref[:, col:col + ch] = up[tm - CONV_HALO:tm, :]
            return jnp.concatenate([y0, y[CONV_HALO:, :]], axis=0)

        n_chunks = D_FF // ch
        acc = h1
        up = up_proj(0)
        acts = []
        for c in range(n_chunks):
            up_next = up_proj(c + 1) if c + 1 < n_chunks else None
            val = conv(up[0], c * ch)
            gate = conv(up[1], D_FF + c * ch)
            acts.append((jax.nn.silu(gate) * val).astype(jnp.bfloat16))
            if len(acts) == DOWN_GROUP or c + 1 == n_chunks:
                lo = (c + 1 - len(acts)) * ch
                act = jnp.concatenate(acts, axis=-1) if len(acts) > 1 else acts[0]
                acc = acc + jnp.dot(act, wdown_ref[lo:(c + 1) * ch, :], preferred_element_type=jnp.float32)
                acts = []
            up = up_next

        h2 = acc
        res = ((h2 * _rms_scale(h2)) * gfin_ref[...]).astype(o_ref.dtype)

        @pl.when(t > 0)
        def _():
            emit(res[0:N_META, :])

        prev_ref[...] = res


def _mix_out(h, attn, sga, pm, w_attn_br, w_out, g_ffn, w_up, conv_w, conv_b, w_down, g_final, seq):
    B, L, D = h.shape
    tm = TM_OUT
    n_tiles = L // tm
    assert D_FF % FF_CHUNK == 0 and L % tm == 0 and pl.cdiv(seq, tm) == n_tiles and tm > N_META
    row = pl.BlockSpec((None, tm, D), lambda b, t: (b, jnp.minimum(t, n_tiles - 1), 0))
    return pl.pallas_call(
        _mix_out_kernel,
        grid=(B, n_tiles + 1),
        in_specs=[
            row, row, row, row,
            _resident(w_attn_br.shape),
            _resident(w_out.shape),
            _resident((1, D)),
            _resident(w_up.shape),
            _resident(conv_w.shape),
            _resident(conv_b.shape),
            _resident(w_down.shape),
            _resident((1, D)),
        ],
        out_specs=pl.BlockSpec((None, tm, D), lambda b, t: (b, jnp.maximum(t - 1, 0), 0)),
        out_shape=jax.ShapeDtypeStruct((B, seq, D), jnp.float32),
        scratch_shapes=[pltpu.VMEM((CONV_HALO, 2 * D_FF), jnp.float32),
                        pltpu.VMEM((tm, D), jnp.float32)],
        compiler_params=pltpu.CompilerParams(
            dimension_semantics=("arbitrary", "arbitrary"), vmem_limit_bytes=VMEM_LIMIT),
        name="mix_out",
    )(h, attn, sga, pm, w_attn_br, w_out, g_ffn, w_up, conv_w, conv_b, w_down, g_final)


def _rope_tables(length):
    pos = np.arange(length, dtype=np.float32)
    inv = (1.0 / (np.float32(ROPE_THETA) ** (np.arange(0, HEAD_DIM, 2, dtype=np.float32) / np.float32(HEAD_DIM))))
    ang = pos[:, None] * inv.astype(np.float32)[None, :]
    cos, sin = np.cos(ang), np.sin(ang)
    cos_t = np.concatenate([cos, cos, cos, cos], axis=-1)
    sin_t = np.concatenate([-sin, sin, -sin, sin], axis=-1)
    return jnp.asarray(cos_t, jnp.float32), jnp.asarray(sin_t, jnp.float32)


def kernel(x, meta_tokens, g_mix, w_in, lam, g_subln, w_pool_grp, pool_scale, w_attn_br, w_pool_br,
           w_out, g_ffn, w_up, conv_w, conv_b, w_down, g_final):
    B, S, D = x.shape
    L = N_META + S
    L_pad = ((L + ROW_ALIGN - 1) // ROW_ALIGN) * ROW_ALIGN
    bf = jnp.bfloat16
    cos_t, sin_t = _rope_tables(L_pad)

    h, qq, k, vt, sga, pm = _mix_in(x, meta_tokens.astype(x.dtype), L_pad, g_mix[0][None], w_in[0].astype(bf),
                                    cos_t, sin_t, w_pool_grp[0].astype(bf), pool_scale[0][None],
                                    w_pool_br[0].astype(bf))
    attn = _diffattn(lam[0], g_subln[0][:, None], qq, k, vt)
    return _mix_out(h, attn, sga, pm, w_attn_br[0].astype(bf), w_out[0].astype(bf), g_ffn[0][None],
                    w_up[0].astype(bf), conv_w[0], conv_b[0][None], w_down[0].astype(bf), g_final[None], S)
```

```python
import functools
import math

import jax
import jax.numpy as jnp
import numpy as np
from jax import lax
from jax.experimental import pallas as pl
from jax.experimental.pallas import tpu as pltpu

D_MODEL = 1024
N_META = 16
N_HEADS = 8
HEAD_DIM = 64
V_DIM = 2 * HEAD_DIM
QK_WIDTH = N_HEADS * 2 * HEAD_DIM
ATTN_WIDTH = N_HEADS * V_DIM
POOL_GROUPS = 4
POOL_WINDOWS = (2, 4, 8, 16)
POOL_WIDTH = 512
POOL_GDIM = POOL_WIDTH // POOL_GROUPS
D_FF = 2816
CONV_WIDTH = 3
ROPE_THETA = 10000.0
EPS = 1e-6
LAM_INIT = 0.8 - 0.6 * math.exp(-0.3 * 0)
LOG2E = math.log2(math.e)

COL_Q = 0
COL_K = QK_WIDTH
COL_V = 2 * QK_WIDTH
COL_U = 2 * QK_WIDTH + ATTN_WIDTH
COL_GA = COL_U + POOL_WIDTH
COL_GP = COL_GA + D_MODEL

LANES = 128
MXU_TILE = 256
POOL_HALO = 16
CONV_HALO = 8
VMEM_LIMIT = 56 * 1024 * 1024

ROW_ALIGN = 768
TM_IN = 768
X_BLOCK = 256
TQ = 768
TK = MXU_TILE
SUM_ROWS = 16
TM_OUT = 704
FF_CHUNK = 256
DOWN_GROUP = 11


def _rms_scale(x):
    return lax.rsqrt(jnp.mean(x * x, axis=-1, keepdims=True) + EPS)


def _resident(shape):
    return pl.BlockSpec(shape, lambda *_: (0,) * len(shape), pipeline_mode=pl.Buffered(1))


def _mix_in_kernel(xf_ref, xa_ref, xb_ref, xc_ref, meta_ref, g_ref, w_in_ref, cos_ref, sin_ref,
                   wgrp_ref, pscale_ref, wpbr_ref,
                   h_ref, qq_ref, k_ref, vt_ref, sga_ref, pm_ref, ubuf_ref, *, seq_len):
    t = pl.program_id(1)
    tm = h_ref.shape[0]
    pos = t * tm + lax.broadcasted_iota(jnp.int32, (tm, 1), 0)

    h_ref[0:N_META, :] = jnp.where(t == 0, meta_ref[...], xf_ref[...])
    h_ref[N_META:N_META + X_BLOCK, :] = xa_ref[...]
    h_ref[N_META + X_BLOCK:N_META + 2 * X_BLOCK, :] = xb_ref[...]
    h_ref[N_META + 2 * X_BLOCK:tm, :] = xc_ref[0:tm - N_META - 2 * X_BLOCK, :]
    h = jnp.where(pos < seq_len, h_ref[...], 0.0)
    h_ref[...] = h
    hn = ((h * _rms_scale(h)) * g_ref[...]).astype(jnp.bfloat16)

    def proj(col, width):
        return jnp.dot(hn, w_in_ref[:, col:col + width], preferred_element_type=jnp.float32)

    cos = cos_ref[...]
    sin = sin_ref[...]
    lane = lax.broadcasted_iota(jnp.int32, (tm, LANES), 1)
    first_half = (lane % HEAD_DIM) < (HEAD_DIM // 2)

    def rope(pj, hh):
        xh = pj[:, hh * LANES:(hh + 1) * LANES]
        partner = jnp.where(first_half,
                            pltpu.roll(xh, LANES - HEAD_DIM // 2, axis=1),
                            pltpu.roll(xh, HEAD_DIM // 2, axis=1))
        return xh * cos + partner * sin

    @pl.when(t == 0)
    def _():
        ubuf_ref[0:POOL_HALO, :] = jnp.zeros((POOL_HALO, POOL_WIDTH), jnp.float32)

    ubuf_ref[POOL_HALO:POOL_HALO + tm, :] = proj(COL_U, POOL_WIDTH)

    pj = proj(COL_Q, QK_WIDTH)
    zeros = jnp.zeros((HEAD_DIM, LANES), qq_ref.dtype)
    for hh in range(N_HEADS):
        qt = (rope(pj, hh) * (LOG2E / math.sqrt(HEAD_DIM))).astype(qq_ref.dtype).T
        for blk in range(tm // LANES):
            src = slice(blk * LANES, (blk + 1) * LANES)
            c0 = slice(2 * blk * LANES, (2 * blk + 1) * LANES)
            c1 = slice((2 * blk + 1) * LANES, (2 * blk + 2) * LANES)
            qq_ref[hh, 0:HEAD_DIM, c0] = qt[0:HEAD_DIM, src]
            qq_ref[hh, HEAD_DIM:V_DIM, c0] = zeros
            qq_ref[hh, 0:HEAD_DIM, c1] = zeros
            qq_ref[hh, HEAD_DIM:V_DIM, c1] = qt[HEAD_DIM:V_DIM, src]

    mixed = []
    for g, w in enumerate(POOL_WINDOWS):
        cols = slice(g * POOL_GDIM, (g + 1) * POOL_GDIM)
        u_g = ubuf_ref[POOL_HALO:POOL_HALO + tm, cols]
        sums = u_g
        for j in range(1, w):
            sums = sums + ubuf_ref[POOL_HALO - j:POOL_HALO - j + tm, cols]
        count = jnp.minimum(pos + 1, w).astype(jnp.float32)
        pooled = (sums / count - u_g).astype(jnp.bfloat16)
        m_g = jnp.dot(pooled, wgrp_ref[g], preferred_element_type=jnp.float32)
        mixed.append((m_g * pscale_ref[:, cols]).astype(jnp.bfloat16))
    pool = jnp.concatenate(mixed, axis=-1)

    pj = proj(COL_K, QK_WIDTH)
    for hh in range(N_HEADS):
        k_ref[:, hh * LANES:(hh + 1) * LANES] = rope(pj, hh).astype(k_ref.dtype)
    pbr = jnp.dot(pool, wpbr_ref[...], preferred_element_type=jnp.float32)

    v = proj(COL_V, ATTN_WIDTH).astype(vt_ref.dtype)
    ones = jnp.ones((SUM_ROWS, TK), vt_ref.dtype)
    for hh in range(N_HEADS):
        for c in range(tm // TK):
            vt_ref[hh, c, 0:V_DIM, :] = v[c * TK:(c + 1) * TK, hh * LANES:(hh + 1) * LANES].T
            vt_ref[hh, c, V_DIM:V_DIM + SUM_ROWS, :] = ones
    pm_ref[...] = (jax.nn.sigmoid(proj(COL_GP, D_MODEL)) * pbr).astype(pm_ref.dtype)
    sga_ref[...] = jax.nn.sigmoid(proj(COL_GA, D_MODEL)).astype(sga_ref.dtype)

    ubuf_ref[0:POOL_HALO, :] = ubuf_ref[tm:tm + POOL_HALO, :]


def _mix_in(x, meta, L, g_mix, w_in, cos_t, sin_t, w_grp, pool_scale, w_pool_br):
    B, S, D = x.shape
    tm = TM_IN
    n_xblk = S // X_BLOCK
    assert L % tm == 0 and S % X_BLOCK == 0 and tm == 3 * X_BLOCK and X_BLOCK % N_META == 0
    assert tm % TK == 0 and tm % LANES == 0
    row = lambda width: pl.BlockSpec((None, tm, width), lambda b, t: (b, t, 0))
    x_head = pl.BlockSpec((None, N_META, D), lambda b, t: (b, jnp.maximum(t * (tm // N_META) - 1, 0), 0))
    x_body = lambda i: pl.BlockSpec((None, X_BLOCK, D),
                                    lambda b, t: (b, jnp.minimum(t * (tm // X_BLOCK) + i, n_xblk - 1), 0))
    bf = jnp.bfloat16
    return pl.pallas_call(
        functools.partial(_mix_in_kernel, seq_len=N_META + S),
        grid=(B, L // tm),
        in_specs=[
            x_head, x_body(0), x_body(1), x_body(2),
            _resident((N_META, D)),
            _resident((1, D)),
            _resident(w_in.shape),
            pl.BlockSpec((tm, LANES), lambda b, t: (t, 0)),
            pl.BlockSpec((tm, LANES), lambda b, t: (t, 0)),
            _resident(w_grp.shape),
            _resident((1, POOL_WIDTH)),
            _resident(w_pool_br.shape),
        ],
        out_specs=[
            row(D),
            pl.BlockSpec((None, N_HEADS, V_DIM, 2 * tm), lambda b, t: (b, 0, 0, t)),
            row(QK_WIDTH),
            pl.BlockSpec((None, N_HEADS, tm // TK, V_DIM + SUM_ROWS, TK), lambda b, t: (b, 0, t, 0, 0)),
            row(D),
            row(D),
        ],
        out_shape=[
            jax.ShapeDtypeStruct((B, L, D), jnp.float32),
            jax.ShapeDtypeStruct((B, N_HEADS, V_DIM, 2 * L), bf),
            jax.ShapeDtypeStruct((B, L, QK_WIDTH), bf),
            jax.ShapeDtypeStruct((B, N_HEADS, L // TK, V_DIM + SUM_ROWS, TK), bf),
            jax.ShapeDtypeStruct((B, L, D), bf),
            jax.ShapeDtypeStruct((B, L, D), bf),
        ],
        scratch_shapes=[pltpu.VMEM((tm + POOL_HALO, POOL_WIDTH), jnp.float32)],
        compiler_params=pltpu.CompilerParams(
            dimension_semantics=("arbitrary", "arbitrary"), vmem_limit_bytes=VMEM_LIMIT),
        name="mix_in",
    )(x, x, x, x, meta, g_mix, w_in, cos_t, sin_t, w_grp, pool_scale, w_pool_br)


def _diffattn_kernel(lam_ref, gsub_ref, qq_ref, k_ref, vt_ref, o_ref, s_ref, cmax_ref, m_ref, acc_ref,
                     *, last_tile_blocks):
    qi = pl.program_id(2)
    n_q = pl.num_programs(2)
    n_qblk = TQ // LANES
    diag_chunks = TQ // TK
    n_slots = diag_chunks

    def run(n_blk):
        hi = 2 * LANES * n_blk
        m_ref[...] = jnp.full(m_ref.shape, -jnp.inf, jnp.float32)
        acc_ref[...] = jnp.zeros(acc_ref.shape, jnp.float32)

        def scores(j, slot, diag=None):
            lo = 0 if diag is None else diag * 2 * TK
            kc = k_ref[pl.ds(pl.multiple_of(j * TK, TK), TK), :]
            s = jnp.dot(kc, qq_ref[:, lo:hi], preferred_element_type=jnp.float32)
            if diag is not None:
                part = min(2 * TK, hi - lo)
                key = lax.broadcasted_iota(jnp.int32, (TK, part), 0)
                lane = lax.broadcasted_iota(jnp.int32, (TK, part), 1)
                visible = key <= (lane // (2 * LANES)) * LANES + lane % LANES
                masked = jnp.where(visible, s[:, 0:part], -jnp.inf)
                s = jnp.concatenate([masked, s[:, part:]], axis=1) if hi - lo > part else masked
            s_ref[slot, :, lo:hi] = s
            cmax_ref[slot, :, lo:hi] = jnp.max(s, axis=0, keepdims=True)

        def consume(j, slot, lo=0):
            m = m_ref[:, lo:hi]
            m_new = jnp.maximum(m, cmax_ref[slot, :, lo:hi])
            alpha = jnp.exp2(m - m_new)
            p = jnp.exp2(s_ref[slot, :, lo:hi] - m_new)
            pv = jnp.dot(vt_ref[j], p.astype(jnp.bfloat16), preferred_element_type=jnp.float32)
            acc_ref[:, lo:hi] = alpha * acc_ref[:, lo:hi] + pv
            m_ref[:, lo:hi] = m_new

        scores(0, 0)

        def block(base):
            for r in range(n_slots):
                scores(base + r + 1, (r + 1) % n_slots)
                consume(base + r, r)

        def pair(i, c):
            block(2 * i * n_slots)
            block((2 * i + 1) * n_slots)
            return c

        lax.fori_loop(0, qi // 2, pair, 0)

        @pl.when(qi % 2 == 1)
        def _():
            block((qi - 1) * n_slots)

        base = qi * n_slots
        live = [d for d in range(diag_chunks) if d * 2 * TK < hi]
        scores(base, 0, diag=0)
        for d in live:
            if d + 1 in live:
                scores(base + d + 1, d + 1, diag=d + 1)
            consume(base + d, d, lo=d * 2 * TK)

        lam = lam_ref[...]
        lam_full = (jnp.exp(jnp.sum(lam[0:1, :] * lam[1:2, :], axis=-1, keepdims=True))
                    - jnp.exp(jnp.sum(lam[2:3, :] * lam[3:4, :], axis=-1, keepdims=True)) + LAM_INIT)
        gsub = gsub_ref[...] * (1.0 - LAM_INIT)
        for blk in range(n_qblk):
            rows = slice(blk * LANES, (blk + 1) * LANES)
            if blk >= n_blk:
                o_ref[rows, :] = jnp.zeros((LANES, V_DIM), o_ref.dtype)
                continue
            c0 = slice(2 * blk * LANES, (2 * blk + 1) * LANES)
            c1 = slice((2 * blk + 1) * LANES, (2 * blk + 2) * LANES)
            o = (acc_ref[0:V_DIM, c0] / acc_ref[V_DIM:V_DIM + 1, c0]
                 - lam_full * (acc_ref[0:V_DIM, c1] / acc_ref[V_DIM:V_DIM + 1, c1]))
            o = o * lax.rsqrt(jnp.mean(o * o, axis=0, keepdims=True) + EPS) * gsub
            o_ref[rows, :] = o.T.astype(o_ref.dtype)

    if last_tile_blocks == n_qblk:
        run(n_qblk)
    else:
        pl.when(qi < n_q - 1)(lambda: run(n_qblk))
        pl.when(qi == n_q - 1)(lambda: run(last_tile_blocks))


def _diffattn(lam, g_subln, qq, k, vt, seq_len):
    B, L, _ = k.shape
    assert TQ % TK == 0 and L % TQ == 0 and TQ % LANES == 0 and L - TQ < seq_len <= L
    last_tile_blocks = pl.cdiv(seq_len - (L - TQ), LANES)
    return pl.pallas_call(
        functools.partial(_diffattn_kernel, last_tile_blocks=last_tile_blocks),
        grid=(B, N_HEADS, L // TQ),
        in_specs=[
            _resident(lam.shape),
            _resident((V_DIM, 1)),
            pl.BlockSpec((None, None, V_DIM, 2 * TQ), lambda b, h, i: (b, h, 0, i)),
            pl.BlockSpec((None, L, V_DIM), lambda b, h, i: (b, 0, h)),
            pl.BlockSpec((None, None, L // TK, V_DIM + SUM_ROWS, TK), lambda b, h, i: (b, h, 0, 0, 0)),
        ],
        out_specs=pl.BlockSpec((None, TQ, V_DIM), lambda b, h, i: (b, i, h)),
        out_shape=jax.ShapeDtypeStruct((B, L, ATTN_WIDTH), jnp.bfloat16),
        scratch_shapes=[pltpu.VMEM((TQ // TK, TK, 2 * TQ), jnp.float32),
                        pltpu.VMEM((TQ // TK, 1, 2 * TQ), jnp.float32),
                        pltpu.VMEM((1, 2 * TQ), jnp.float32),
                        pltpu.VMEM((V_DIM + SUM_ROWS, 2 * TQ), jnp.float32)],
        compiler_params=pltpu.CompilerParams(
            dimension_semantics=("arbitrary", "arbitrary", "arbitrary"), vmem_limit_bytes=VMEM_LIMIT),
        name="diffattn",
    )(lam, g_subln, qq, k, vt)


def _mix_out_kernel(h_ref, attn_ref, sga_ref, pm_ref, wabr_ref, wout_ref, gffn_ref, wup_ref,
                    convw_ref, convb_ref, wdown_ref, gfin_ref, o_ref, halo_ref, prev_ref):
    t = pl.program_id(1)
    n_tiles = pl.num_programs(1) - 1
    tm = h_ref.shape[0]
    ch = FF_CHUNK

    def emit(first_rows_of_next):
        o_ref[0:tm - N_META, :] = prev_ref[N_META:tm, :]
        o_ref[tm - N_META:tm, :] = first_rows_of_next

    @pl.when(t == n_tiles)
    def _():
        emit(jnp.zeros((N_META, D_MODEL), o_ref.dtype))

    @pl.when(t < n_tiles)
    def _():
        abr = jnp.dot(attn_ref[...], wabr_ref[...], preferred_element_type=jnp.float32)
        merged = sga_ref[...].astype(jnp.float32) * abr + pm_ref[...].astype(jnp.float32)
        h1 = h_ref[...] + jnp.dot(merged.astype(jnp.bfloat16), wout_ref[...],
                                  preferred_element_type=jnp.float32)
        hn = ((h1 * _rms_scale(h1)) * gffn_ref[...]).astype(jnp.bfloat16)

        @pl.when(t == 0)
        def _():
            halo_ref[...] = jnp.zeros_like(halo_ref)

        def up_proj(c):
            return tuple(jnp.dot(hn, wup_ref[:, col:col + ch], preferred_element_type=jnp.float32)
                         for col in (c * ch, D_FF + c * ch))

        def conv(up, col):
            taps = [convw_ref[CONV_WIDTH - 1 - j:CONV_WIDTH - j, col:col + ch] for j in range(CONV_WIDTH)]
            bias = convb_ref[:, col:col + ch]
            y = bias + taps[0] * up
            for j in range(1, CONV_WIDTH):
                y = y + taps[j] * pltpu.roll(up, j, axis=0)
            ext = jnp.concatenate([halo_ref[:, col:col + ch], up[0:CONV_HALO, :]], axis=0)
            y0 = bias + taps[0] * ext[CONV_HALO:2 * CONV_HALO, :]
            for j in range(1, CONV_WIDTH):
                y0 = y0 + taps[j] * ext[CONV_HALO - j:2 * CONV_HALO - j, :]
            halo_ref[:, col:col + ch] = up[tm - CONV_HALO:tm, :]
            return jnp.concatenate([y0, y[CONV_HALO:, :]], axis=0)

        n_chunks = D_FF // ch
        acc = h1
        up = up_proj(0)
        acts = []
        for c in range(n_chunks):
            up_next = up_proj(c + 1) if c + 1 < n_chunks else None
            val = conv(up[0], c * ch)
            gate = conv(up[1], D_FF + c * ch)
            acts.append((jax.nn.silu(gate) * val).astype(jnp.bfloat16))
            if len(acts) == DOWN_GROUP or c + 1 == n_chunks:
                lo = (c + 1 - len(acts)) * ch
                act = jnp.concatenate(acts, axis=-1) if len(acts) > 1 else acts[0]
                acc = acc + jnp.dot(act, wdown_ref[lo:(c + 1) * ch, :], preferred_element_type=jnp.float32)
                acts = []
            up = up_next

        h2 = acc
        res = ((h2 * _rms_scale(h2)) * gfin_ref[...]).astype(o_ref.dtype)

        @pl.when(t > 0)
        def _():
            emit(res[0:N_META, :])

        prev_ref[...] = res


def _mix_out(h, attn, sga, pm, w_attn_br, w_out, g_ffn, w_up, conv_w, conv_b, w_down, g_final, seq):
    B, L, D = h.shape
    tm = TM_OUT
    n_tiles = L // tm
    assert D_FF % FF_CHUNK == 0 and L % tm == 0 and pl.cdiv(seq, tm) == n_tiles and tm > N_META
    row = pl.BlockSpec((None, tm, D), lambda b, t: (b, jnp.minimum(t, n_tiles - 1), 0))
    return pl.pallas_call(
        _mix_out_kernel,
        grid=(B, n_tiles + 1),
        in_specs=[
            row, row, row, row,
            _resident(w_attn_br.shape),
            _resident(w_out.shape),
            _resident((1, D)),
            _resident(w_up.shape),
            _resident(conv_w.shape),
            _resident(conv_b.shape),
            _resident(w_down.shape),
            _resident((1, D)),
        ],
        out_specs=pl.BlockSpec((None, tm, D), lambda b, t: (b, jnp.maximum(t - 1, 0), 0)),
        out_shape=jax.ShapeDtypeStruct((B, seq, D), jnp.float32),
        scratch_shapes=[pltpu.VMEM((CONV_HALO, 2 * D_FF), jnp.float32),
                        pltpu.VMEM((tm, D), jnp.float32)],
        compiler_params=pltpu.CompilerParams(
            dimension_semantics=("arbitrary", "arbitrary"), vmem_limit_bytes=VMEM_LIMIT),
        name="mix_out",
    )(h, attn, sga, pm, w_attn_br, w_out, g_ffn, w_up, conv_w, conv_b, w_down, g_final)


def _rope_tables(length):
    pos = np.arange(length, dtype=np.float32)
    inv = (1.0 / (np.float32(ROPE_THETA) ** (np.arange(0, HEAD_DIM, 2, dtype=np.float32) / np.float32(HEAD_DIM))))
    ang = pos[:, None] * inv.astype(np.float32)[None, :]
    cos, sin = np.cos(ang), np.sin(ang)
    cos_t = np.concatenate([cos, cos, cos, cos], axis=-1)
    sin_t = np.concatenate([-sin, sin, -sin, sin], axis=-1)
    return jnp.asarray(cos_t, jnp.float32), jnp.asarray(sin_t, jnp.float32)


def kernel(x, meta_tokens, g_mix, w_in, lam, g_subln, w_pool_grp, pool_scale, w_attn_br, w_pool_br,
           w_out, g_ffn, w_up, conv_w, conv_b, w_down, g_final):
    B, S, D = x.shape
    L = N_META + S
    L_pad = ((L + ROW_ALIGN - 1) // ROW_ALIGN) * ROW_ALIGN
    bf = jnp.bfloat16
    cos_t, sin_t = _rope_tables(L_pad)

    h, qq, k, vt, sga, pm = _mix_in(x, meta_tokens.astype(x.dtype), L_pad, g_mix[0][None], w_in[0].astype(bf),
                                    cos_t, sin_t, w_pool_grp[0].astype(bf), pool_scale[0][None],
                                    w_pool_br[0].astype(bf))
    attn = _diffattn(lam[0], g_subln[0][:, None], qq, k, vt, L)
    return _mix_out(h, attn, sga, pm, w_attn_br[0].astype(bf), w_out[0].astype(bf), g_ffn[0][None],
                    w_up[0].astype(bf), conv_w[0], conv_b[0][None], w_down[0].astype(bf), g_final[None], S)
```

```python
import functools
import math

import jax
import jax.numpy as jnp
import numpy as np
from jax import lax
from jax.experimental import pallas as pl
from jax.experimental.pallas import tpu as pltpu

D_MODEL = 1024
N_META = 16
N_HEADS = 8
HEAD_DIM = 64
V_DIM = 2 * HEAD_DIM
QK_WIDTH = N_HEADS * 2 * HEAD_DIM
ATTN_WIDTH = N_HEADS * V_DIM
POOL_GROUPS = 4
POOL_WINDOWS = (2, 4, 8, 16)
POOL_WIDTH = 512
POOL_GDIM = POOL_WIDTH // POOL_GROUPS
D_FF = 2816
CONV_WIDTH = 3
ROPE_THETA = 10000.0
EPS = 1e-6
LAM_INIT = 0.8 - 0.6 * math.exp(-0.3 * 0)
LOG2E = math.log2(math.e)

COL_Q = 0
COL_K = QK_WIDTH
COL_V = 2 * QK_WIDTH
COL_U = 2 * QK_WIDTH + ATTN_WIDTH
COL_GA = COL_U + POOL_WIDTH
COL_GP = COL_GA + D_MODEL

LANES = 128
MXU_TILE = 256
POOL_HALO = 16
CONV_HALO = 8
VMEM_LIMIT = 56 * 1024 * 1024

ROW_ALIGN = 768
TM_IN = 768
X_BLOCK = 256
TQ = 768
TK = MXU_TILE
SUM_ROWS = 16
TM_OUT = 704
FF_CHUNK = 256
DOWN_GROUP = 11


def _rms_scale(x):
    return lax.rsqrt(jnp.mean(x * x, axis=-1, keepdims=True) + EPS)


def _resident(shape):
    return pl.BlockSpec(shape, lambda *_: (0,) * len(shape), pipeline_mode=pl.Buffered(1))


def _mix_in_kernel(xf_ref, xa_ref, xb_ref, xc_ref, meta_ref, g_ref, w_in_ref, cos_ref, sin_ref,
                   wgrp_ref, pscale_ref, wpbr_ref,
                   h_ref, qq_ref, k_ref, vt_ref, sga_ref, pm_ref, ubuf_ref, *, seq_len):
    t = pl.program_id(1)
    tm = h_ref.shape[0]
    pos = t * tm + lax.broadcasted_iota(jnp.int32, (tm, 1), 0)

    h_ref[0:N_META, :] = jnp.where(t == 0, meta_ref[...], xf_ref[...])
    h_ref[N_META:N_META + X_BLOCK, :] = xa_ref[...]
    h_ref[N_META + X_BLOCK:N_META + 2 * X_BLOCK, :] = xb_ref[...]
    h_ref[N_META + 2 * X_BLOCK:tm, :] = xc_ref[0:tm - N_META - 2 * X_BLOCK, :]
    h = jnp.where(pos < seq_len, h_ref[...], 0.0)
    h_ref[...] = h
    hn = ((h * _rms_scale(h)) * g_ref[...]).astype(jnp.bfloat16)

    def proj(col, width):
        return jnp.dot(hn, w_in_ref[:, col:col + width], preferred_element_type=jnp.float32)

    cos = cos_ref[...]
    sin = sin_ref[...]
    lane = lax.broadcasted_iota(jnp.int32, (tm, LANES), 1)
    first_half = (lane % HEAD_DIM) < (HEAD_DIM // 2)

    def rope(pj, hh):
        xh = pj[:, hh * LANES:(hh + 1) * LANES]
        partner = jnp.where(first_half,
                            pltpu.roll(xh, LANES - HEAD_DIM // 2, axis=1),
                            pltpu.roll(xh, HEAD_DIM // 2, axis=1))
        return xh * cos + partner * sin

    @pl.when(t == 0)
    def _():
        ubuf_ref[0:POOL_HALO, :] = jnp.zeros((POOL_HALO, POOL_WIDTH), jnp.float32)

    ubuf_ref[POOL_HALO:POOL_HALO + tm, :] = proj(COL_U, POOL_WIDTH)

    pj = proj(COL_Q, QK_WIDTH)
    zeros = jnp.zeros((HEAD_DIM, LANES), qq_ref.dtype)
    for hh in range(N_HEADS):
        qt = (rope(pj, hh) * (LOG2E / math.sqrt(HEAD_DIM))).astype(qq_ref.dtype).T
        for blk in range(tm // LANES):
            src = slice(blk * LANES, (blk + 1) * LANES)
            c0 = slice(2 * blk * LANES, (2 * blk + 1) * LANES)
            c1 = slice((2 * blk + 1) * LANES, (2 * blk + 2) * LANES)
            qq_ref[hh, 0:HEAD_DIM, c0] = qt[0:HEAD_DIM, src]
            qq_ref[hh, HEAD_DIM:V_DIM, c0] = zeros
            qq_ref[hh, 0:HEAD_DIM, c1] = zeros
            qq_ref[hh, HEAD_DIM:V_DIM, c1] = qt[HEAD_DIM:V_DIM, src]

    mixed = []
    for g, w in enumerate(POOL_WINDOWS):
        cols = slice(g * POOL_GDIM, (g + 1) * POOL_GDIM)
        u_g = ubuf_ref[POOL_HALO:POOL_HALO + tm, cols]
        sums = u_g
        for j in range(1, w):
            sums = sums + ubuf_ref[POOL_HALO - j:POOL_HALO - j + tm, cols]
        count = jnp.minimum(pos + 1, w).astype(jnp.float32)
        pooled = (sums / count - u_g).astype(jnp.bfloat16)
        m_g = jnp.dot(pooled, wgrp_ref[g], preferred_element_type=jnp.float32)
        mixed.append((m_g * pscale_ref[:, cols]).astype(jnp.bfloat16))
    pool = jnp.concatenate(mixed, axis=-1)

    pj = proj(COL_K, QK_WIDTH)
    for hh in range(N_HEADS):
        k_ref[:, hh * LANES:(hh + 1) * LANES] = rope(pj, hh).astype(k_ref.dtype)
    pbr = jnp.dot(pool, wpbr_ref[...], preferred_element_type=jnp.float32)

    v = proj(COL_V, ATTN_WIDTH).astype(vt_ref.dtype)
    ones = jnp.ones((SUM_ROWS, TK), vt_ref.dtype)
    for hh in range(N_HEADS):
        for c in range(tm // TK):
            vt_ref[hh, c, 0:V_DIM, :] = v[c * TK:(c + 1) * TK, hh * LANES:(hh + 1) * LANES].T
            vt_ref[hh, c, V_DIM:V_DIM + SUM_ROWS, :] = ones
    pm_ref[...] = (jax.nn.sigmoid(proj(COL_GP, D_MODEL)) * pbr).astype(pm_ref.dtype)
    sga_ref[...] = jax.nn.sigmoid(proj(COL_GA, D_MODEL)).astype(sga_ref.dtype)

    ubuf_ref[0:POOL_HALO, :] = ubuf_ref[tm:tm + POOL_HALO, :]


def _mix_in(x, meta, L, g_mix, w_in, cos_t, sin_t, w_grp, pool_scale, w_pool_br):
    B, S, D = x.shape
    tm = TM_IN
    n_xblk = S // X_BLOCK
    assert L % tm == 0 and S % X_BLOCK == 0 and tm == 3 * X_BLOCK and X_BLOCK % N_META == 0
    assert tm % TK == 0 and tm % LANES == 0
    row = lambda width: pl.BlockSpec((None, tm, width), lambda b, t: (b, t, 0))
    x_head = pl.BlockSpec((None, N_META, D), lambda b, t: (b, jnp.maximum(t * (tm // N_META) - 1, 0), 0))
    x_body = lambda i: pl.BlockSpec((None, X_BLOCK, D),
                                    lambda b, t: (b, jnp.minimum(t * (tm // X_BLOCK) + i, n_xblk - 1), 0))
    bf = jnp.bfloat16
    return pl.pallas_call(
        functools.partial(_mix_in_kernel, seq_len=N_META + S),
        grid=(B, L // tm),
        in_specs=[
            x_head, x_body(0), x_body(1), x_body(2),
            _resident((N_META, D)),
            _resident((1, D)),
            _resident(w_in.shape),
            pl.BlockSpec((tm, LANES), lambda b, t: (t, 0)),
            pl.BlockSpec((tm, LANES), lambda b, t: (t, 0)),
            _resident(w_grp.shape),
            _resident((1, POOL_WIDTH)),
            _resident(w_pool_br.shape),
        ],
        out_specs=[
            row(D),
            pl.BlockSpec((None, N_HEADS, V_DIM, 2 * tm), lambda b, t: (b, 0, 0, t)),
            row(QK_WIDTH),
            pl.BlockSpec((None, N_HEADS, tm // TK, V_DIM + SUM_ROWS, TK), lambda b, t: (b, 0, t, 0, 0)),
            row(D),
            row(D),
        ],
        out_shape=[
            jax.ShapeDtypeStruct((B, L, D), jnp.float32),
            jax.ShapeDtypeStruct((B, N_HEADS, V_DIM, 2 * L), bf),
            jax.ShapeDtypeStruct((B, L, QK_WIDTH), bf),
            jax.ShapeDtypeStruct((B, N_HEADS, L // TK, V_DIM + SUM_ROWS, TK), bf),
            jax.ShapeDtypeStruct((B, L, D), bf),
            jax.ShapeDtypeStruct((B, L, D), bf),
        ],
        scratch_shapes=[pltpu.VMEM((tm + POOL_HALO, POOL_WIDTH), jnp.float32)],
        compiler_params=pltpu.CompilerParams(
            dimension_semantics=("arbitrary", "arbitrary"), vmem_limit_bytes=VMEM_LIMIT),
        name="mix_in",
    )(x, x, x, x, meta, g_mix, w_in, cos_t, sin_t, w_grp, pool_scale, w_pool_br)


def _diffattn_kernel(lam_ref, gsub_ref, qq_ref, k_ref, vt_ref, o_ref, s_ref, cmax_ref, m_ref, acc_ref,
                     *, last_tile_blocks):
    qi = pl.program_id(2)
    n_q = pl.num_programs(2)
    n_qblk = TQ // LANES
    diag_chunks = TQ // TK
    n_slots = diag_chunks

    def run(n_blk):
        hi = 2 * LANES * n_blk
        m_ref[...] = jnp.full(m_ref.shape, -jnp.inf, jnp.float32)
        acc_ref[...] = jnp.zeros(acc_ref.shape, jnp.float32)

        def scores(j, slot, diag=None, diagonal_if=None):
            lo = 0 if diag is None else diag * 2 * TK
            kc = k_ref[pl.ds(pl.multiple_of(j * TK, TK), TK), :]
            s = jnp.dot(kc, qq_ref[:, lo:hi], preferred_element_type=jnp.float32)
            if diag is not None or diagonal_if is not None:
                part = min(2 * TK, hi - lo)
                key = lax.broadcasted_iota(jnp.int32, (TK, part), 0)
                lane = lax.broadcasted_iota(jnp.int32, (TK, part), 1)
                qry = (lane // (2 * LANES)) * LANES + lane % LANES
                if diagonal_if is not None:
                    qry = qry + jnp.where(diagonal_if, 0, TK)
                visible = key <= qry
                masked = jnp.where(visible, s[:, 0:part], -jnp.inf)
                s = jnp.concatenate([masked, s[:, part:]], axis=1) if hi - lo > part else masked
            s_ref[slot, :, lo:hi] = s
            cmax_ref[slot, :, lo:hi] = jnp.max(s, axis=0, keepdims=True)

        def consume(j, slot, lo=0):
            m = m_ref[:, lo:hi]
            m_new = jnp.maximum(m, cmax_ref[slot, :, lo:hi])
            alpha = jnp.exp2(m - m_new)
            p = jnp.exp2(s_ref[slot, :, lo:hi] - m_new)
            pv = jnp.dot(vt_ref[j], p.astype(jnp.bfloat16), preferred_element_type=jnp.float32)
            acc_ref[:, lo:hi] = alpha * acc_ref[:, lo:hi] + pv
            m_ref[:, lo:hi] = m_new

        scores(0, 0, diagonal_if=qi == 0)

        def block(b, last):
            base = b * n_slots
            for r in range(n_slots):
                if r + 1 < n_slots or last is False:
                    nxt = {}
                else:
                    nxt = dict(diag=0) if last is True else dict(diagonal_if=last)
                scores(base + r + 1, (r + 1) % n_slots, **nxt)
                consume(base + r, r)

        def pair(i, c):
            block(2 * i, last=False)
            block(2 * i + 1, last=2 * i + 2 == qi)
            return c

        lax.fori_loop(0, qi // 2, pair, 0)

        @pl.when(qi % 2 == 1)
        def _():
            block(qi - 1, last=True)

        base = qi * n_slots
        live = [d for d in range(diag_chunks) if d * 2 * TK < hi]
        for d in live:
            if d + 1 in live:
                scores(base + d + 1, d + 1, diag=d + 1)
            consume(base + d, d, lo=d * 2 * TK)

        lam = lam_ref[...]
        lam_full = (jnp.exp(jnp.sum(lam[0:1, :] * lam[1:2, :], axis=-1, keepdims=True))
                    - jnp.exp(jnp.sum(lam[2:3, :] * lam[3:4, :], axis=-1, keepdims=True)) + LAM_INIT)
        gsub = gsub_ref[...] * (1.0 - LAM_INIT)
        for blk in range(n_qblk):
            rows = slice(blk * LANES, (blk + 1) * LANES)
            if blk >= n_blk:
                o_ref[rows, :] = jnp.zeros((LANES, V_DIM), o_ref.dtype)
                continue
            c0 = slice(2 * blk * LANES, (2 * blk + 1) * LANES)
            c1 = slice((2 * blk + 1) * LANES, (2 * blk + 2) * LANES)
            o = (acc_ref[0:V_DIM, c0] / acc_ref[V_DIM:V_DIM + 1, c0]
                 - lam_full * (acc_ref[0:V_DIM, c1] / acc_ref[V_DIM:V_DIM + 1, c1]))
            o = o * lax.rsqrt(jnp.mean(o * o, axis=0, keepdims=True) + EPS) * gsub
            o_ref[rows, :] = o.T.astype(o_ref.dtype)

    if last_tile_blocks == n_qblk:
        run(n_qblk)
    else:
        pl.when(qi < n_q - 1)(lambda: run(n_qblk))
        pl.when(qi == n_q - 1)(lambda: run(last_tile_blocks))


def _diffattn(lam, g_subln, qq, k, vt, seq_len):
    B, L, _ = k.shape
    assert TQ % TK == 0 and L % TQ == 0 and TQ % LANES == 0 and L - TQ < seq_len <= L
    last_tile_blocks = pl.cdiv(seq_len - (L - TQ), LANES)
    return pl.pallas_call(
        functools.partial(_diffattn_kernel, last_tile_blocks=last_tile_blocks),
        grid=(B, N_HEADS, L // TQ),
        in_specs=[
            _resident(lam.shape),
            _resident((V_DIM, 1)),
            pl.BlockSpec((None, None, V_DIM, 2 * TQ), lambda b, h, i: (b, h, 0, i)),
            pl.BlockSpec((None, L, V_DIM), lambda b, h, i: (b, 0, h)),
            pl.BlockSpec((None, None, L // TK, V_DIM + SUM_ROWS, TK), lambda b, h, i: (b, h, 0, 0, 0)),
        ],
        out_specs=pl.BlockSpec((None, TQ, V_DIM), lambda b, h, i: (b, i, h)),
        out_shape=jax.ShapeDtypeStruct((B, L, ATTN_WIDTH), jnp.bfloat16),
        scratch_shapes=[pltpu.VMEM((TQ // TK, TK, 2 * TQ), jnp.float32),
                        pltpu.VMEM((TQ // TK, 1, 2 * TQ), jnp.float32),
                        pltpu.VMEM((1, 2 * TQ), jnp.float32),
                        pltpu.VMEM((V_DIM + SUM_ROWS, 2 * TQ), jnp.float32)],
        compiler_params=pltpu.CompilerParams(
            dimension_semantics=("arbitrary", "arbitrary", "arbitrary"), vmem_limit_bytes=VMEM_LIMIT),
        name="diffattn",
    )(lam, g_subln, qq, k, vt)


def _mix_out_kernel(h_ref, attn_ref, sga_ref, pm_ref, wabr_ref, wout_ref, gffn_ref, wup_ref,
                    convw_ref, convb_ref, wdown_ref, gfin_ref, o_ref, halo_ref, prev_ref):
    t = pl.program_id(1)
    n_tiles = pl.num_programs(1) - 1
    tm = h_ref.shape[0]
    ch = FF_CHUNK

    def emit(first_rows_of_next):
        o_ref[0:tm - N_META, :] = prev_ref[N_META:tm, :]
        o_ref[tm - N_META:tm, :] = first_rows_of_next

    @pl.when(t == n_tiles)
    def _():
        emit(jnp.zeros((N_META, D_MODEL), o_ref.dtype))

    @pl.when(t < n_tiles)
    def _():
        abr = jnp.dot(attn_ref[...], wabr_ref[...], preferred_element_type=jnp.float32)
        merged = sga_ref[...].astype(jnp.float32) * abr + pm_ref[...].astype(jnp.float32)
        h1 = h_ref[...] + jnp.dot(merged.astype(jnp.bfloat16), wout_ref[...],
                                  preferred_element_type=jnp.float32)
        hn = ((h1 * _rms_scale(h1)) * gffn_ref[...]).astype(jnp.bfloat16)

        @pl.when(t == 0)
        def _():
            halo_ref[...] = jnp.zeros_like(halo_ref)

        def up_proj(c):
            return tuple(jnp.dot(hn, wup_ref[:, col:col + ch], preferred_element_type=jnp.float32)
                         for col in (c * ch, D_FF + c * ch))

        def conv(up, col):
            taps = [convw_ref[CONV_WIDTH - 1 - j:CONV_WIDTH - j, col:col + ch] for j in range(CONV_WIDTH)]
            bias = convb_ref[:, col:col + ch]
            y = bias + taps[0] * up
            for j in range(1, CONV_WIDTH):
                y = y + taps[j] * pltpu.roll(up, j, axis=0)
            ext = jnp.concatenate([halo_ref[:, col:col + ch], up[0:CONV_HALO, :]], axis=0)
            y0 = bias + taps[0] * ext[CONV_HALO:2 * CONV_HALO, :]
            for j in range(1, CONV_WIDTH):
                y0 = y0 + taps[j] * ext[CONV_HALO - j:2 * CONV_HALO - j, :]
            halo_ref[:, col:col + ch] = up[tm - CONV_HALO:tm, :]
            return jnp.concatenate([y0, y[CONV_HALO:, :]], axis=0)

        n_chunks = D_FF // ch
        acc = h1
        up = up_proj(0)
        acts = []
        for c in range(n_chunks):
            up_next = up_proj(c + 1) if c + 1 < n_chunks else None
            val = conv(up[0], c * ch)
            gate = conv(up[1], D_FF + c * ch)
            acts.append((jax.nn.silu(gate) * val).astype(jnp.bfloat16))
            if len(acts) == DOWN_GROUP or c + 1 == n_chunks:
                lo = (c + 1 - len(acts)) * ch
                act = jnp.concatenate(acts, axis=-1) if len(acts) > 1 else acts[0]
                acc = acc + jnp.dot(act, wdown_ref[lo:(c + 1) * ch, :], preferred_element_type=jnp.float32)
                acts = []
            up = up_next

        h2 = acc
        res = ((h2 * _rms_scale(h2)) * gfin_ref[...]).astype(o_ref.dtype)

        @pl.when(t > 0)
        def _():
            emit(res[0:N_META, :])

        prev_ref[...] = res


def _mix_out(h, attn, sga, pm, w_attn_br, w_out, g_ffn, w_up, conv_w, conv_b, w_down, g_final, seq):
    B, L, D = h.shape
    tm = TM_OUT
    n_tiles = L // tm
    assert D_FF % FF_CHUNK == 0 and L % tm == 0 and pl.cdiv(seq, tm) == n_tiles and tm > N_META
    row = pl.BlockSpec((None, tm, D), lambda b, t: (b, jnp.minimum(t, n_tiles - 1), 0))
    return pl.pallas_call(
        _mix_out_kernel,
        grid=(B, n_tiles + 1),
        in_specs=[
            row, row, row, row,
            _resident(w_attn_br.shape),
            _resident(w_out.shape),
            _resident((1, D)),
            _resident(w_up.shape),
            _resident(conv_w.shape),
            _resident(conv_b.shape),
            _resident(w_down.shape),
            _resident((1, D)),
        ],
        out_specs=pl.BlockSpec((None, tm, D), lambda b, t: (b, jnp.maximum(t - 1, 0), 0)),
        out_shape=jax.ShapeDtypeStruct((B, seq, D), jnp.float32),
        scratch_shapes=[pltpu.VMEM((CONV_HALO, 2 * D_FF), jnp.float32),
                        pltpu.VMEM((tm, D), jnp.float32)],
        compiler_params=pltpu.CompilerParams(
            dimension_semantics=("arbitrary", "arbitrary"), vmem_limit_bytes=VMEM_LIMIT),
        name="mix_out",
    )(h, attn, sga, pm, w_attn_br, w_out, g_ffn, w_up, conv_w, conv_b, w_down, g_final)


def _rope_tables(length):
    pos = np.arange(length, dtype=np.float32)
    inv = (1.0 / (np.float32(ROPE_THETA) ** (np.arange(0, HEAD_DIM, 2, dtype=np.float32) / np.float32(HEAD_DIM))))
    ang = pos[:, None] * inv.astype(np.float32)[None, :]
    cos, sin = np.cos(ang), np.sin(ang)
    cos_t = np.concatenate([cos, cos, cos, cos], axis=-1)
    sin_t = np.concatenate([-sin, sin, -sin, sin], axis=-1)
    return jnp.asarray(cos_t, jnp.float32), jnp.asarray(sin_t, jnp.float32)


def kernel(x, meta_tokens, g_mix, w_in, lam, g_subln, w_pool_grp, pool_scale, w_attn_br, w_pool_br,
           w_out, g_ffn, w_up, conv_w, conv_b, w_down, g_final):
    B, S, D = x.shape
    L = N_META + S
    L_pad = ((L + ROW_ALIGN - 1) // ROW_ALIGN) * ROW_ALIGN
    bf = jnp.bfloat16
    cos_t, sin_t = _rope_tables(L_pad)

    h, qq, k, vt, sga, pm = _mix_in(x, meta_tokens.astype(x.dtype), L_pad, g_mix[0][None], w_in[0].astype(bf),
                                    cos_t, sin_t, w_pool_grp[0].astype(bf), pool_scale[0][None],
                                    w_pool_br[0].astype(bf))
    attn = _diffattn(lam[0], g_subln[0][:, None], qq, k, vt, L)
    return _mix_out(h, attn, sga, pm, w_attn_br[0].astype(bf), w_out[0].astype(bf), g_ffn[0][None],
                    w_up[0].astype(bf), conv_w[0], conv_b[0][None], w_down[0].astype(bf), g_final[None], S)
```

```python
import functools
import math

import jax
import jax.numpy as jnp
import numpy as np
from jax import lax
from jax.experimental import pallas as pl
from jax.experimental.pallas import tpu as pltpu

D_MODEL = 1024
N_META = 16
N_HEADS = 8
HEAD_DIM = 64
V_DIM = 2 * HEAD_DIM
QK_WIDTH = N_HEADS * 2 * HEAD_DIM
ATTN_WIDTH = N_HEADS * V_DIM
POOL_GROUPS = 4
POOL_WINDOWS = (2, 4, 8, 16)
POOL_WIDTH = 512
POOL_GDIM = POOL_WIDTH // POOL_GROUPS
D_FF = 2816
CONV_WIDTH = 3
ROPE_THETA = 10000.0
EPS = 1e-6
LAM_INIT = 0.8 - 0.6 * math.exp(-0.3 * 0)
LOG2E = math.log2(math.e)

COL_Q = 0
COL_K = QK_WIDTH
COL_V = 2 * QK_WIDTH
COL_U = 2 * QK_WIDTH + ATTN_WIDTH
COL_GA = COL_U + POOL_WIDTH
COL_GP = COL_GA + D_MODEL

LANES = 128
MXU_TILE = 256
POOL_HALO = 16
CONV_HALO = 8
VMEM_LIMIT = 56 * 1024 * 1024

ROW_ALIGN = 768
TM_IN = 768
X_BLOCK = 256
TQ = 768
TK = MXU_TILE
SUM_ROWS = 16
TM_OUT = 704
FF_CHUNK = 256
DOWN_GROUP = 11


def _aligned(start, multiple):
    return start if isinstance(start, int) else pl.multiple_of(start, multiple)


def _rms_scale(x):
    return lax.rsqrt(jnp.mean(x * x, axis=-1, keepdims=True) + EPS)


def _resident(shape):
    return pl.BlockSpec(shape, lambda *_: (0,) * len(shape), pipeline_mode=pl.Buffered(1))


def _mix_in_kernel(xf_ref, xa_ref, xb_ref, xc_ref, meta_ref, g_ref, w_in_ref, cos_ref, sin_ref,
                   wgrp_ref, pscale_ref, wpbr_ref,
                   h_ref, qq_ref, k_ref, vt_ref, sga_ref, pm_ref, ubuf_ref, *, seq_len):
    t = pl.program_id(1)
    tm = h_ref.shape[0]
    pos = t * tm + lax.broadcasted_iota(jnp.int32, (tm, 1), 0)

    h_ref[0:N_META, :] = jnp.where(t == 0, meta_ref[...], xf_ref[...])
    h_ref[N_META:N_META + X_BLOCK, :] = xa_ref[...]
    h_ref[N_META + X_BLOCK:N_META + 2 * X_BLOCK, :] = xb_ref[...]
    h_ref[N_META + 2 * X_BLOCK:tm, :] = xc_ref[0:tm - N_META - 2 * X_BLOCK, :]
    h = jnp.where(pos < seq_len, h_ref[...], 0.0)
    h_ref[...] = h
    hn = ((h * _rms_scale(h)) * g_ref[...]).astype(jnp.bfloat16)

    def proj(col, width):
        return jnp.dot(hn, w_in_ref[:, col:col + width], preferred_element_type=jnp.float32)

    cos = cos_ref[...]
    sin = sin_ref[...]
    lane = lax.broadcasted_iota(jnp.int32, (tm, LANES), 1)
    first_half = (lane % HEAD_DIM) < (HEAD_DIM // 2)

    def rope(pj, hh):
        xh = pj[:, hh * LANES:(hh + 1) * LANES]
        partner = jnp.where(first_half,
                            pltpu.roll(xh, LANES - HEAD_DIM // 2, axis=1),
                            pltpu.roll(xh, HEAD_DIM // 2, axis=1))
        return xh * cos + partner * sin

    @pl.when(t == 0)
    def _():
        ubuf_ref[0:POOL_HALO, :] = jnp.zeros((POOL_HALO, POOL_WIDTH), jnp.float32)

    ubuf_ref[POOL_HALO:POOL_HALO + tm, :] = proj(COL_U, POOL_WIDTH)

    pj = proj(COL_Q, QK_WIDTH)
    zeros = jnp.zeros((HEAD_DIM, LANES), qq_ref.dtype)
    for hh in range(N_HEADS):
        qt = (rope(pj, hh) * (LOG2E / math.sqrt(HEAD_DIM))).astype(qq_ref.dtype).T
        for blk in range(tm // LANES):
            src = slice(blk * LANES, (blk + 1) * LANES)
            c0 = slice(2 * blk * LANES, (2 * blk + 1) * LANES)
            c1 = slice((2 * blk + 1) * LANES, (2 * blk + 2) * LANES)
            qq_ref[hh, 0:HEAD_DIM, c0] = qt[0:HEAD_DIM, src]
            qq_ref[hh, HEAD_DIM:V_DIM, c0] = zeros
            qq_ref[hh, 0:HEAD_DIM, c1] = zeros
            qq_ref[hh, HEAD_DIM:V_DIM, c1] = qt[HEAD_DIM:V_DIM, src]

    mixed = []
    for g, w in enumerate(POOL_WINDOWS):
        cols = slice(g * POOL_GDIM, (g + 1) * POOL_GDIM)
        u_g = ubuf_ref[POOL_HALO:POOL_HALO + tm, cols]
        sums = u_g
        for j in range(1, w):
            sums = sums + ubuf_ref[POOL_HALO - j:POOL_HALO - j + tm, cols]
        count = jnp.minimum(pos + 1, w).astype(jnp.float32)
        pooled = (sums / count - u_g).astype(jnp.bfloat16)
        m_g = jnp.dot(pooled, wgrp_ref[g], preferred_element_type=jnp.float32)
        mixed.append((m_g * pscale_ref[:, cols]).astype(jnp.bfloat16))
    pool = jnp.concatenate(mixed, axis=-1)

    pj = proj(COL_K, QK_WIDTH)
    for hh in range(N_HEADS):
        k_ref[:, hh * LANES:(hh + 1) * LANES] = rope(pj, hh).astype(k_ref.dtype)
    pbr = jnp.dot(pool, wpbr_ref[...], preferred_element_type=jnp.float32)

    v = proj(COL_V, ATTN_WIDTH).astype(vt_ref.dtype)
    ones = jnp.ones((SUM_ROWS, TK), vt_ref.dtype)
    for hh in range(N_HEADS):
        for c in range(tm // TK):
            vt_ref[hh, c, 0:V_DIM, :] = v[c * TK:(c + 1) * TK, hh * LANES:(hh + 1) * LANES].T
            vt_ref[hh, c, V_DIM:V_DIM + SUM_ROWS, :] = ones
    pm_ref[...] = (jax.nn.sigmoid(proj(COL_GP, D_MODEL)) * pbr).astype(pm_ref.dtype)
    sga_ref[...] = jax.nn.sigmoid(proj(COL_GA, D_MODEL)).astype(sga_ref.dtype)

    ubuf_ref[0:POOL_HALO, :] = ubuf_ref[tm:tm + POOL_HALO, :]


def _mix_in(x, meta, L, g_mix, w_in, cos_t, sin_t, w_grp, pool_scale, w_pool_br):
    B, S, D = x.shape
    tm = TM_IN
    n_xblk = S // X_BLOCK
    assert L % tm == 0 and S % X_BLOCK == 0 and tm == 3 * X_BLOCK and X_BLOCK % N_META == 0
    assert tm % TK == 0 and tm % LANES == 0 and tm == TQ
    row = lambda width: pl.BlockSpec((None, tm, width), lambda b, t: (b, t, 0))
    x_head = pl.BlockSpec((None, N_META, D), lambda b, t: (b, jnp.maximum(t * (tm // N_META) - 1, 0), 0))
    x_body = lambda i: pl.BlockSpec((None, X_BLOCK, D),
                                    lambda b, t: (b, jnp.minimum(t * (tm // X_BLOCK) + i, n_xblk - 1), 0))
    bf = jnp.bfloat16
    return pl.pallas_call(
        functools.partial(_mix_in_kernel, seq_len=N_META + S),
        grid=(B, L // tm),
        in_specs=[
            x_head, x_body(0), x_body(1), x_body(2),
            _resident((N_META, D)),
            _resident((1, D)),
            _resident(w_in.shape),
            pl.BlockSpec((tm, LANES), lambda b, t: (t, 0)),
            pl.BlockSpec((tm, LANES), lambda b, t: (t, 0)),
            _resident(w_grp.shape),
            _resident((1, POOL_WIDTH)),
            _resident(w_pool_br.shape),
        ],
        out_specs=[
            row(D),
            pl.BlockSpec((None, N_HEADS, None, V_DIM, 2 * tm), lambda b, t: (b, 0, t, 0, 0)),
            row(QK_WIDTH),
            pl.BlockSpec((None, N_HEADS, tm // TK, V_DIM + SUM_ROWS, TK), lambda b, t: (b, 0, t, 0, 0)),
            row(D),
            row(D),
        ],
        out_shape=[
            jax.ShapeDtypeStruct((B, L, D), jnp.float32),
            jax.ShapeDtypeStruct((B, N_HEADS, L // tm, V_DIM, 2 * tm), bf),
            jax.ShapeDtypeStruct((B, L, QK_WIDTH), bf),
            jax.ShapeDtypeStruct((B, N_HEADS, L // TK, V_DIM + SUM_ROWS, TK), bf),
            jax.ShapeDtypeStruct((B, L, D), bf),
            jax.ShapeDtypeStruct((B, L, D), bf),
        ],
        scratch_shapes=[pltpu.VMEM((tm + POOL_HALO, POOL_WIDTH), jnp.float32)],
        compiler_params=pltpu.CompilerParams(
            dimension_semantics=("arbitrary", "arbitrary"), vmem_limit_bytes=VMEM_LIMIT),
        name="mix_in",
    )(x, x, x, x, meta, g_mix, w_in, cos_t, sin_t, w_grp, pool_scale, w_pool_br)


def _diffattn_kernel(lam_ref, gsub_ref, qq_ref, k_ref, vt_ref, o_ref, s_ref, cmax_ref, m_ref, acc_ref,
                     *, last_tile_blocks):
    n_q = qq_ref.shape[0]
    n_qblk = TQ // LANES
    diag_chunks = TQ // TK
    n_slots = diag_chunks

    def run(qi, n_blk):
        hi = 2 * LANES * n_blk
        m_ref[...] = jnp.full(m_ref.shape, -jnp.inf, jnp.float32)
        acc_ref[...] = jnp.zeros(acc_ref.shape, jnp.float32)

        def scores(j, slot, diag=None):
            lo = 0 if diag is None else diag * 2 * TK
            kc = k_ref[pl.ds(_aligned(j * TK, TK), TK), :]
            s = jnp.dot(kc, qq_ref[qi, :, lo:hi], preferred_element_type=jnp.float32)
            if diag is not None:
                part = min(2 * TK, hi - lo)
                key = lax.broadcasted_iota(jnp.int32, (TK, part), 0)
                lane = lax.broadcasted_iota(jnp.int32, (TK, part), 1)
                visible = key <= (lane // (2 * LANES)) * LANES + lane % LANES
                masked = jnp.where(visible, s[:, 0:part], -jnp.inf)
                s = jnp.concatenate([masked, s[:, part:]], axis=1) if hi - lo > part else masked
            s_ref[slot, :, lo:hi] = s
            cmax_ref[slot, :, lo:hi] = jnp.max(s, axis=0, keepdims=True)

        def consume(j, slot, lo=0):
            m = m_ref[:, lo:hi]
            m_new = jnp.maximum(m, cmax_ref[slot, :, lo:hi])
            alpha = jnp.exp2(m - m_new)
            p = jnp.exp2(s_ref[slot, :, lo:hi] - m_new)
            pv = jnp.dot(vt_ref[j], p.astype(jnp.bfloat16), preferred_element_type=jnp.float32)
            acc_ref[:, lo:hi] = alpha * acc_ref[:, lo:hi] + pv
            m_ref[:, lo:hi] = m_new

        scores(0, 0)

        def block(base):
            for r in range(n_slots):
                scores(base + r + 1, (r + 1) % n_slots)
                consume(base + r, r)

        def pair(i, c):
            block(2 * i * n_slots)
            block((2 * i + 1) * n_slots)
            return c

        lax.fori_loop(0, qi // 2, pair, 0)
        if isinstance(qi, int):
            if qi % 2 == 1:
                block((qi - 1) * n_slots)
        else:
            pl.when(qi % 2 == 1)(lambda: block((qi - 1) * n_slots))

        base = qi * n_slots
        live = [d for d in range(diag_chunks) if d * 2 * TK < hi]
        scores(base, 0, diag=0)
        for d in live:
            if d + 1 in live:
                scores(base + d + 1, d + 1, diag=d + 1)
            consume(base + d, d, lo=d * 2 * TK)

        lam = lam_ref[...]
        lam_full = (jnp.exp(jnp.sum(lam[0:1, :] * lam[1:2, :], axis=-1, keepdims=True))
                    - jnp.exp(jnp.sum(lam[2:3, :] * lam[3:4, :], axis=-1, keepdims=True)) + LAM_INIT)
        gsub = gsub_ref[...] * (1.0 - LAM_INIT)
        for blk in range(n_qblk):
            rows = pl.ds(_aligned(qi * TQ + blk * LANES, LANES), LANES)
            if blk >= n_blk:
                o_ref[rows, :] = jnp.zeros((LANES, V_DIM), o_ref.dtype)
                continue
            c0 = slice(2 * blk * LANES, (2 * blk + 1) * LANES)
            c1 = slice((2 * blk + 1) * LANES, (2 * blk + 2) * LANES)
            o = (acc_ref[0:V_DIM, c0] / acc_ref[V_DIM:V_DIM + 1, c0]
                 - lam_full * (acc_ref[0:V_DIM, c1] / acc_ref[V_DIM:V_DIM + 1, c1]))
            o = o * lax.rsqrt(jnp.mean(o * o, axis=0, keepdims=True) + EPS) * gsub
            o_ref[rows, :] = o.T.astype(o_ref.dtype)

    def tile(qi, c):
        run(qi, n_qblk)
        return c

    lax.fori_loop(0, n_q - 1, tile, 0)
    run(n_q - 1, last_tile_blocks)


def _diffattn(lam, g_subln, qq, k, vt, seq_len):
    B, L, _ = k.shape
    assert TQ % TK == 0 and L % TQ == 0 and TQ % LANES == 0 and L - TQ < seq_len <= L
    last_tile_blocks = pl.cdiv(seq_len - (L - TQ), LANES)
    return pl.pallas_call(
        functools.partial(_diffattn_kernel, last_tile_blocks=last_tile_blocks),
        grid=(B, N_HEADS),
        in_specs=[
            _resident(lam.shape),
            _resident((V_DIM, 1)),
            pl.BlockSpec((None, None, L // TQ, V_DIM, 2 * TQ), lambda b, h: (b, h, 0, 0, 0)),
            pl.BlockSpec((None, L, V_DIM), lambda b, h: (b, 0, h)),
            pl.BlockSpec((None, None, L // TK, V_DIM + SUM_ROWS, TK), lambda b, h: (b, h, 0, 0, 0)),
        ],
        out_specs=pl.BlockSpec((None, L, V_DIM), lambda b, h: (b, 0, h)),
        out_shape=jax.ShapeDtypeStruct((B, L, ATTN_WIDTH), jnp.bfloat16),
        scratch_shapes=[pltpu.VMEM((TQ // TK, TK, 2 * TQ), jnp.float32),
                        pltpu.VMEM((TQ // TK, 1, 2 * TQ), jnp.float32),
                        pltpu.VMEM((1, 2 * TQ), jnp.float32),
                        pltpu.VMEM((V_DIM + SUM_ROWS, 2 * TQ), jnp.float32)],
        compiler_params=pltpu.CompilerParams(
            dimension_semantics=("arbitrary", "arbitrary"), vmem_limit_bytes=VMEM_LIMIT),
        name="diffattn",
    )(lam, g_subln, qq, k, vt)


def _mix_out_kernel(h_ref, attn_ref, sga_ref, pm_ref, wabr_ref, wout_ref, gffn_ref, wup_ref,
                    convw_ref, convb_ref, wdown_ref, gfin_ref, o_ref, halo_ref, prev_ref):
    t = pl.program_id(1)
    n_tiles = pl.num_programs(1) - 1
    tm = h_ref.shape[0]
    ch = FF_CHUNK

    def emit(first_rows_of_next):
        o_ref[0:tm - N_META, :] = prev_ref[N_META:tm, :]
        o_ref[tm - N_META:tm, :] = first_rows_of_next

    @pl.when(t == n_tiles)
    def _():
        emit(jnp.zeros((N_META, D_MODEL), o_ref.dtype))

    @pl.when(t < n_tiles)
    def _():
        abr = jnp.dot(attn_ref[...], wabr_ref[...], preferred_element_type=jnp.float32)
        merged = sga_ref[...].astype(jnp.float32) * abr + pm_ref[...].astype(jnp.float32)
        h1 = h_ref[...] + jnp.dot(merged.astype(jnp.bfloat16), wout_ref[...],
                                  preferred_element_type=jnp.float32)
        hn = ((h1 * _rms_scale(h1)) * gffn_ref[...]).astype(jnp.bfloat16)

        @pl.when(t == 0)
        def _():
            halo_ref[...] = jnp.zeros_like(halo_ref)

        def up_proj(c):
            return tuple(jnp.dot(hn, wup_ref[:, col:col + ch], preferred_element_type=jnp.float32)
                         for col in (c * ch, D_FF + c * ch))

        def conv(up, col):
            taps = [convw_ref[CONV_WIDTH - 1 - j:CONV_WIDTH - j, col:col + ch] for j in range(CONV_WIDTH)]
            bias = convb_ref[:, col:col + ch]
            y = bias + taps[0] * up
            for j in range(1, CONV_WIDTH):
                y = y + taps[j] * pltpu.roll(up, j, axis=0)
            ext = jnp.concatenate([halo_ref[:, col:col + ch], up[0:CONV_HALO, :]], axis=0)
            y0 = bias + taps[0] * ext[CONV_HALO:2 * CONV_HALO, :]
            for j in range(1, CONV_WIDTH):
                y0 = y0 + taps[j] * ext[CONV_HALO - j:2 * CONV_HALO - j, :]
            halo_ref[:, col:col + ch] = up[tm - CONV_HALO:tm, :]
            return jnp.concatenate([y0, y[CONV_HALO:, :]], axis=0)

        n_chunks = D_FF // ch
        acc = h1
        up = up_proj(0)
        acts = []
        for c in range(n_chunks):
            up_next = up_proj(c + 1) if c + 1 < n_chunks else None
            val = conv(up[0], c * ch)
            gate = conv(up[1], D_FF + c * ch)
            acts.append((jax.nn.silu(gate) * val).astype(jnp.bfloat16))
            if len(acts) == DOWN_GROUP or c + 1 == n_chunks:
                lo = (c + 1 - len(acts)) * ch
                act = jnp.concatenate(acts, axis=-1) if len(acts) > 1 else acts[0]
                acc = acc + jnp.dot(act, wdown_ref[lo:(c + 1) * ch, :], preferred_element_type=jnp.float32)
                acts = []
            up = up_next

        h2 = acc
        res = ((h2 * _rms_scale(h2)) * gfin_ref[...]).astype(o_ref.dtype)

        @pl.when(t > 0)
        def _():
            emit(res[0:N_META, :])

        prev_ref[...] = res


def _mix_out(h, attn, sga, pm, w_attn_br, w_out, g_ffn, w_up, conv_w, conv_b, w_down, g_final, seq):
    B, L, D = h.shape
    tm = TM_OUT
    n_tiles = L // tm
    assert D_FF % FF_CHUNK == 0 and L % tm == 0 and pl.cdiv(seq, tm) == n_tiles and tm > N_META
    row = pl.BlockSpec((None, tm, D), lambda b, t: (b, jnp.minimum(t, n_tiles - 1), 0))
    return pl.pallas_call(
        _mix_out_kernel,
        grid=(B, n_tiles + 1),
        in_specs=[
            row, row, row, row,
            _resident(w_attn_br.shape),
            _resident(w_out.shape),
            _resident((1, D)),
            _resident(w_up.shape),
            _resident(conv_w.shape),
            _resident(conv_b.shape),
            _resident(w_down.shape),
            _resident((1, D)),
        ],
        out_specs=pl.BlockSpec((None, tm, D), lambda b, t: (b, jnp.maximum(t - 1, 0), 0)),
        out_shape=jax.ShapeDtypeStruct((B, seq, D), jnp.float32),
        scratch_shapes=[pltpu.VMEM((CONV_HALO, 2 * D_FF), jnp.float32),
                        pltpu.VMEM((tm, D), jnp.float32)],
        compiler_params=pltpu.CompilerParams(
            dimension_semantics=("arbitrary", "arbitrary"), vmem_limit_bytes=VMEM_LIMIT),
        name="mix_out",
    )(h, attn, sga, pm, w_attn_br, w_out, g_ffn, w_up, conv_w, conv_b, w_down, g_final)


def _rope_tables(length):
    pos = np.arange(length, dtype=np.float32)
    inv = (1.0 / (np.float32(ROPE_THETA) ** (np.arange(0, HEAD_DIM, 2, dtype=np.float32) / np.float32(HEAD_DIM))))
    ang = pos[:, None] * inv.astype(np.float32)[None, :]
    cos, sin = np.cos(ang), np.sin(ang)
    cos_t = np.concatenate([cos, cos, cos, cos], axis=-1)
    sin_t = np.concatenate([-sin, sin, -sin, sin], axis=-1)
    return jnp.asarray(cos_t, jnp.float32), jnp.asarray(sin_t, jnp.float32)


def kernel(x, meta_tokens, g_mix, w_in, lam, g_subln, w_pool_grp, pool_scale, w_attn_br, w_pool_br,
           w_out, g_ffn, w_up, conv_w, conv_b, w_down, g_final):
    B, S, D = x.shape
    L = N_META + S
    L_pad = ((L + ROW_ALIGN - 1) // ROW_ALIGN) * ROW_ALIGN
    bf = jnp.bfloat16
    cos_t, sin_t = _rope_tables(L_pad)

    h, qq, k, vt, sga, pm = _mix_in(x, meta_tokens.astype(x.dtype), L_pad, g_mix[0][None], w_in[0].astype(bf),
                                    cos_t, sin_t, w_pool_grp[0].astype(bf), pool_scale[0][None],
                                    w_pool_br[0].astype(bf))
    attn = _diffattn(lam[0], g_subln[0][:, None], qq, k, vt, L)
    return _mix_out(h, attn, sga, pm, w_attn_br[0].astype(bf), w_out[0].astype(bf), g_ffn[0][None],
                    w_up[0].astype(bf), conv_w[0], conv_b[0][None], w_down[0].astype(bf), g_final[None], S)
```

```python
import functools
import math

import jax
import jax.numpy as jnp
import numpy as np
from jax import lax
from jax.experimental import pallas as pl
from jax.experimental.pallas import tpu as pltpu

D_MODEL = 1024
N_META = 16
N_HEADS = 8
HEAD_DIM = 64
V_DIM = 2 * HEAD_DIM
QK_WIDTH = N_HEADS * 2 * HEAD_DIM
ATTN_WIDTH = N_HEADS * V_DIM
POOL_GROUPS = 4
POOL_WINDOWS = (2, 4, 8, 16)
POOL_WIDTH = 512
POOL_GDIM = POOL_WIDTH // POOL_GROUPS
D_FF = 2816
CONV_WIDTH = 3
ROPE_THETA = 10000.0
EPS = 1e-6
LAM_INIT = 0.8 - 0.6 * math.exp(-0.3 * 0)
LOG2E = math.log2(math.e)

COL_Q = 0
COL_K = QK_WIDTH
COL_V = 2 * QK_WIDTH
COL_U = 2 * QK_WIDTH + ATTN_WIDTH
COL_GA = COL_U + POOL_WIDTH
COL_GP = COL_GA + D_MODEL

LANES = 128
MXU_TILE = 256
POOL_HALO = 16
CONV_HALO = 8
VMEM_LIMIT = 56 * 1024 * 1024

ROW_ALIGN = 768
TM_IN = 768
X_BLOCK = 256
TQ = 768
TK = MXU_TILE
SUM_ROWS = 16
TM_OUT = 704
FF_CHUNK = 256
DOWN_GROUP = 11


def _aligned(start, multiple):
    return start if isinstance(start, int) else pl.multiple_of(start, multiple)


def _rms_scale(x):
    return lax.rsqrt(jnp.mean(x * x, axis=-1, keepdims=True) + EPS)


def _resident(shape):
    return pl.BlockSpec(shape, lambda *_: (0,) * len(shape), pipeline_mode=pl.Buffered(1))


def _mix_in_kernel(xf_ref, xa_ref, xb_ref, xc_ref, meta_ref, g_ref, w_in_ref, cos_ref, sin_ref,
                   wgrp_ref, pscale_ref, wpbr_ref,
                   h_ref, qq_ref, k_ref, vt_ref, sga_ref, pm_ref, ubuf_ref, *, seq_len):
    t = pl.program_id(1)
    tm = h_ref.shape[0]
    pos = t * tm + lax.broadcasted_iota(jnp.int32, (tm, 1), 0)

    h_ref[0:N_META, :] = jnp.where(t == 0, meta_ref[...], xf_ref[...])
    h_ref[N_META:N_META + X_BLOCK, :] = xa_ref[...]
    h_ref[N_META + X_BLOCK:N_META + 2 * X_BLOCK, :] = xb_ref[...]
    h_ref[N_META + 2 * X_BLOCK:tm, :] = xc_ref[0:tm - N_META - 2 * X_BLOCK, :]
    h = jnp.where(pos < seq_len, h_ref[...], 0.0)
    h_ref[...] = h
    hn = ((h * _rms_scale(h)) * g_ref[...]).astype(jnp.bfloat16)

    def proj(col, width):
        return jnp.dot(hn, w_in_ref[:, col:col + width], preferred_element_type=jnp.float32)

    cos = cos_ref[...]
    sin = sin_ref[...]
    lane = lax.broadcasted_iota(jnp.int32, (tm, LANES), 1)
    first_half = (lane % HEAD_DIM) < (HEAD_DIM // 2)

    def rope(pj, hh):
        xh = pj[:, hh * LANES:(hh + 1) * LANES]
        partner = jnp.where(first_half,
                            pltpu.roll(xh, LANES - HEAD_DIM // 2, axis=1),
                            pltpu.roll(xh, HEAD_DIM // 2, axis=1))
        return xh * cos + partner * sin

    @pl.when(t == 0)
    def _():
        ubuf_ref[0:POOL_HALO, :] = jnp.zeros((POOL_HALO, POOL_WIDTH), jnp.float32)

    ubuf_ref[POOL_HALO:POOL_HALO + tm, :] = proj(COL_U, POOL_WIDTH)

    pj = proj(COL_Q, QK_WIDTH)
    zeros = jnp.zeros((HEAD_DIM, LANES), qq_ref.dtype)
    for hh in range(N_HEADS):
        qt = (rope(pj, hh) * (LOG2E / math.sqrt(HEAD_DIM))).astype(qq_ref.dtype).T
        for blk in range(tm // LANES):
            src = slice(blk * LANES, (blk + 1) * LANES)
            c0 = slice(2 * blk * LANES, (2 * blk + 1) * LANES)
            c1 = slice((2 * blk + 1) * LANES, (2 * blk + 2) * LANES)
            qq_ref[hh, 0:HEAD_DIM, c0] = qt[0:HEAD_DIM, src]
            qq_ref[hh, HEAD_DIM:V_DIM, c0] = zeros
            qq_ref[hh, 0:HEAD_DIM, c1] = zeros
            qq_ref[hh, HEAD_DIM:V_DIM, c1] = qt[HEAD_DIM:V_DIM, src]

    mixed = []
    for g, w in enumerate(POOL_WINDOWS):
        cols = slice(g * POOL_GDIM, (g + 1) * POOL_GDIM)
        u_g = ubuf_ref[POOL_HALO:POOL_HALO + tm, cols]
        sums = u_g
        for j in range(1, w):
            sums = sums + ubuf_ref[POOL_HALO - j:POOL_HALO - j + tm, cols]
        count = jnp.minimum(pos + 1, w).astype(jnp.float32)
        pooled = (sums / count - u_g).astype(jnp.bfloat16)
        m_g = jnp.dot(pooled, wgrp_ref[g], preferred_element_type=jnp.float32)
        mixed.append((m_g * pscale_ref[:, cols]).astype(jnp.bfloat16))
    pool = jnp.concatenate(mixed, axis=-1)

    pj = proj(COL_K, QK_WIDTH)
    for hh in range(N_HEADS):
        k_ref[:, hh * LANES:(hh + 1) * LANES] = rope(pj, hh).astype(k_ref.dtype)
    pbr = jnp.dot(pool, wpbr_ref[...], preferred_element_type=jnp.float32)

    v = proj(COL_V, ATTN_WIDTH).astype(vt_ref.dtype)
    ones = jnp.ones((SUM_ROWS, TK), vt_ref.dtype)
    for hh in range(N_HEADS):
        for c in range(tm // TK):
            vt_ref[hh, c, 0:V_DIM, :] = v[c * TK:(c + 1) * TK, hh * LANES:(hh + 1) * LANES].T
            vt_ref[hh, c, V_DIM:V_DIM + SUM_ROWS, :] = ones
    pm_ref[...] = (jax.nn.sigmoid(proj(COL_GP, D_MODEL)) * pbr).astype(pm_ref.dtype)
    sga_ref[...] = jax.nn.sigmoid(proj(COL_GA, D_MODEL)).astype(sga_ref.dtype)

    ubuf_ref[0:POOL_HALO, :] = ubuf_ref[tm:tm + POOL_HALO, :]


def _mix_in(x, meta, L, g_mix, w_in, cos_t, sin_t, w_grp, pool_scale, w_pool_br):
    B, S, D = x.shape
    tm = TM_IN
    n_xblk = S // X_BLOCK
    assert L % tm == 0 and S % X_BLOCK == 0 and tm == 3 * X_BLOCK and X_BLOCK % N_META == 0
    assert tm % TK == 0 and tm % LANES == 0 and tm == TQ
    row = lambda width: pl.BlockSpec((None, tm, width), lambda b, t: (b, t, 0))
    x_head = pl.BlockSpec((None, N_META, D), lambda b, t: (b, jnp.maximum(t * (tm // N_META) - 1, 0), 0))
    x_body = lambda i: pl.BlockSpec((None, X_BLOCK, D),
                                    lambda b, t: (b, jnp.minimum(t * (tm // X_BLOCK) + i, n_xblk - 1), 0))
    bf = jnp.bfloat16
    return pl.pallas_call(
        functools.partial(_mix_in_kernel, seq_len=N_META + S),
        grid=(B, L // tm),
        in_specs=[
            x_head, x_body(0), x_body(1), x_body(2),
            _resident((N_META, D)),
            _resident((1, D)),
            _resident(w_in.shape),
            pl.BlockSpec((tm, LANES), lambda b, t: (t, 0)),
            pl.BlockSpec((tm, LANES), lambda b, t: (t, 0)),
            _resident(w_grp.shape),
            _resident((1, POOL_WIDTH)),
            _resident(w_pool_br.shape),
        ],
        out_specs=[
            row(D),
            pl.BlockSpec((None, N_HEADS, None, V_DIM, 2 * tm), lambda b, t: (b, 0, t, 0, 0)),
            row(QK_WIDTH),
            pl.BlockSpec((None, N_HEADS, tm // TK, V_DIM + SUM_ROWS, TK), lambda b, t: (b, 0, t, 0, 0)),
            row(D),
            row(D),
        ],
        out_shape=[
            jax.ShapeDtypeStruct((B, L, D), jnp.float32),
            jax.ShapeDtypeStruct((B, N_HEADS, L // tm, V_DIM, 2 * tm), bf),
            jax.ShapeDtypeStruct((B, L, QK_WIDTH), bf),
            jax.ShapeDtypeStruct((B, N_HEADS, L // TK, V_DIM + SUM_ROWS, TK), bf),
            jax.ShapeDtypeStruct((B, L, D), bf),
            jax.ShapeDtypeStruct((B, L, D), bf),
        ],
        scratch_shapes=[pltpu.VMEM((tm + POOL_HALO, POOL_WIDTH), jnp.float32)],
        compiler_params=pltpu.CompilerParams(
            dimension_semantics=("arbitrary", "arbitrary"), vmem_limit_bytes=VMEM_LIMIT),
        name="mix_in",
    )(x, x, x, x, meta, g_mix, w_in, cos_t, sin_t, w_grp, pool_scale, w_pool_br)


def _diffattn_kernel(lam_ref, gsub_ref, qq_ref, k_ref, vt_ref, o_ref, s_ref, cmax_ref, m_ref, acc_ref,
                     *, last_tile_blocks):
    n_q = qq_ref.shape[0]
    n_qblk = TQ // LANES
    diag_chunks = TQ // TK
    n_slots = diag_chunks

    def run(qi, n_blk):
        hi = 2 * LANES * n_blk
        m_ref[...] = jnp.full(m_ref.shape, -jnp.inf, jnp.float32)
        acc_ref[...] = jnp.zeros(acc_ref.shape, jnp.float32)

        def scores(j, slot, diag=None, w=1):
            lo = 0 if diag is None else diag * 2 * TK
            kc = k_ref[pl.ds(_aligned(j * TK, TK), w * TK), :]
            s = jnp.dot(kc, qq_ref[qi, :, lo:hi], preferred_element_type=jnp.float32)
            if diag is not None:
                part = min(2 * TK, hi - lo)
                key = lax.broadcasted_iota(jnp.int32, (TK, part), 0)
                lane = lax.broadcasted_iota(jnp.int32, (TK, part), 1)
                visible = key <= (lane // (2 * LANES)) * LANES + lane % LANES
                masked = jnp.where(visible, s[:, 0:part], -jnp.inf)
                s = jnp.concatenate([masked, s[:, part:]], axis=1) if hi - lo > part else masked
            for i in range(w):
                s_ref[slot + i, :, lo:hi] = s[i * TK:(i + 1) * TK, :]
            cmax_ref[slot, :, lo:hi] = jnp.max(s, axis=0, keepdims=True)

        def consume(j, slot, lo=0, w=1):
            m = m_ref[:, lo:hi]
            m_new = jnp.maximum(m, cmax_ref[slot, :, lo:hi])
            alpha = jnp.exp2(m - m_new)
            p = [jnp.exp2(s_ref[slot + i, :, lo:hi] - m_new).astype(jnp.bfloat16) for i in range(w)]
            vt = [vt_ref[j + i] for i in range(w)]
            p, vt = (jnp.concatenate(p, axis=0), jnp.concatenate(vt, axis=1)) if w > 1 else (p[0], vt[0])
            pv = jnp.dot(vt, p, preferred_element_type=jnp.float32)
            acc_ref[:, lo:hi] = alpha * acc_ref[:, lo:hi] + pv
            m_ref[:, lo:hi] = m_new

        def group_scores(c, b):
            scores(c * n_slots, b * n_slots, w=n_slots)

        def group_consume(c, b):
            consume(c * n_slots, b * n_slots, w=n_slots)

        def when(cond, fn):
            if isinstance(cond, bool):
                if cond:
                    fn()
            else:
                pl.when(cond)(fn)

        when(qi > 0, lambda: group_scores(0, 0))

        def pair(i, c):
            group_scores(2 * i + 1, 1)
            group_consume(2 * i, 0)
            group_scores(2 * i + 2, 0)
            group_consume(2 * i + 1, 1)
            return c

        lax.fori_loop(0, (qi - 1) // 2, pair, 0)

        def last_one():
            group_consume(qi - 1, 0)

        def last_two():
            group_scores(qi - 1, 1)
            group_consume(qi - 2, 0)
            group_consume(qi - 1, 1)

        odd = qi % 2 == 1
        when(odd if isinstance(qi, int) else jnp.logical_and(qi > 0, odd), last_one)
        when((qi > 0 and not odd) if isinstance(qi, int) else jnp.logical_and(qi > 0, jnp.logical_not(odd)), last_two)

        base = qi * n_slots
        live = [d for d in range(diag_chunks) if d * 2 * TK < hi]
        scores(base, 0, diag=0)
        for d in live:
            if d + 1 in live:
                scores(base + d + 1, d + 1, diag=d + 1)
            consume(base + d, d, lo=d * 2 * TK)

        lam = lam_ref[...]
        lam_full = (jnp.exp(jnp.sum(lam[0:1, :] * lam[1:2, :], axis=-1, keepdims=True))
                    - jnp.exp(jnp.sum(lam[2:3, :] * lam[3:4, :], axis=-1, keepdims=True)) + LAM_INIT)
        gsub = gsub_ref[...] * (1.0 - LAM_INIT)
        for blk in range(n_qblk):
            rows = pl.ds(_aligned(qi * TQ + blk * LANES, LANES), LANES)
            if blk >= n_blk:
                o_ref[rows, :] = jnp.zeros((LANES, V_DIM), o_ref.dtype)
                continue
            c0 = slice(2 * blk * LANES, (2 * blk + 1) * LANES)
            c1 = slice((2 * blk + 1) * LANES, (2 * blk + 2) * LANES)
            o = (acc_ref[0:V_DIM, c0] / acc_ref[V_DIM:V_DIM + 1, c0]
                 - lam_full * (acc_ref[0:V_DIM, c1] / acc_ref[V_DIM:V_DIM + 1, c1]))
            o = o * lax.rsqrt(jnp.mean(o * o, axis=0, keepdims=True) + EPS) * gsub
            o_ref[rows, :] = o.T.astype(o_ref.dtype)

    def tile(qi, c):
        run(qi, n_qblk)
        return c

    lax.fori_loop(0, n_q - 1, tile, 0)
    run(n_q - 1, last_tile_blocks)


def _diffattn(lam, g_subln, qq, k, vt, seq_len):
    B, L, _ = k.shape
    assert TQ % TK == 0 and L % TQ == 0 and TQ % LANES == 0 and L - TQ < seq_len <= L
    last_tile_blocks = pl.cdiv(seq_len - (L - TQ), LANES)
    return pl.pallas_call(
        functools.partial(_diffattn_kernel, last_tile_blocks=last_tile_blocks),
        grid=(B, N_HEADS),
        in_specs=[
            _resident(lam.shape),
            _resident((V_DIM, 1)),
            pl.BlockSpec((None, None, L // TQ, V_DIM, 2 * TQ), lambda b, h: (b, h, 0, 0, 0)),
            pl.BlockSpec((None, L, V_DIM), lambda b, h: (b, 0, h)),
            pl.BlockSpec((None, None, L // TK, V_DIM + SUM_ROWS, TK), lambda b, h: (b, h, 0, 0, 0)),
        ],
        out_specs=pl.BlockSpec((None, L, V_DIM), lambda b, h: (b, 0, h)),
        out_shape=jax.ShapeDtypeStruct((B, L, ATTN_WIDTH), jnp.bfloat16),
        scratch_shapes=[pltpu.VMEM((2 * TQ // TK, TK, 2 * TQ), jnp.float32),
                        pltpu.VMEM((2 * TQ // TK, 1, 2 * TQ), jnp.float32),
                        pltpu.VMEM((1, 2 * TQ), jnp.float32),
                        pltpu.VMEM((V_DIM + SUM_ROWS, 2 * TQ), jnp.float32)],
        compiler_params=pltpu.CompilerParams(
            dimension_semantics=("arbitrary", "arbitrary"), vmem_limit_bytes=VMEM_LIMIT),
        name="diffattn",
    )(lam, g_subln, qq, k, vt)


def _mix_out_kernel(h_ref, attn_ref, sga_ref, pm_ref, wabr_ref, wout_ref, gffn_ref, wup_ref,
                    convw_ref, convb_ref, wdown_ref, gfin_ref, o_ref, halo_ref, prev_ref):
    t = pl.program_id(1)
    n_tiles = pl.num_programs(1) - 1
    tm = h_ref.shape[0]
    ch = FF_CHUNK

    def emit(first_rows_of_next):
        o_ref[0:tm - N_META, :] = prev_ref[N_META:tm, :]
        o_ref[tm - N_META:tm, :] = first_rows_of_next

    @pl.when(t == n_tiles)
    def _():
        emit(jnp.zeros((N_META, D_MODEL), o_ref.dtype))

    @pl.when(t < n_tiles)
    def _():
        abr = jnp.dot(attn_ref[...], wabr_ref[...], preferred_element_type=jnp.float32)
        merged = sga_ref[...].astype(jnp.float32) * abr + pm_ref[...].astype(jnp.float32)
        h1 = h_ref[...] + jnp.dot(merged.astype(jnp.bfloat16), wout_ref[...],
                                  preferred_element_type=jnp.float32)
        hn = ((h1 * _rms_scale(h1)) * gffn_ref[...]).astype(jnp.bfloat16)

        @pl.when(t == 0)
        def _():
            halo_ref[...] = jnp.zeros_like(halo_ref)

        def up_proj(c):
            return tuple(jnp.dot(hn, wup_ref[:, col:col + ch], preferred_element_type=jnp.float32)
                         for col in (c * ch, D_FF + c * ch))

        def conv(up, col):
            taps = [convw_ref[CONV_WIDTH - 1 - j:CONV_WIDTH - j, col:col + ch] for j in range(CONV_WIDTH)]
            bias = convb_ref[:, col:col + ch]
            y = bias + taps[0] * up
            for j in range(1, CONV_WIDTH):
                y = y + taps[j] * pltpu.roll(up, j, axis=0)
            ext = jnp.concatenate([halo_ref[:, col:col + ch], up[0:CONV_HALO, :]], axis=0)
            y0 = bias + taps[0] * ext[CONV_HALO:2 * CONV_HALO, :]
            for j in range(1, CONV_WIDTH):
                y0 = y0 + taps[j] * ext[CONV_HALO - j:2 * CONV_HALO - j, :]
            halo_ref[:, col:col + ch] = up[tm - CONV_HALO:tm, :]
            return jnp.concatenate([y0, y[CONV_HALO:, :]], axis=0)

        n_chunks = D_FF // ch
        acc = h1
        up = up_proj(0)
        acts = []
        for c in range(n_chunks):
            up_next = up_proj(c + 1) if c + 1 < n_chunks else None
            val = conv(up[0], c * ch)
            gate = conv(up[1], D_FF + c * ch)
            acts.append((jax.nn.silu(gate) * val).astype(jnp.bfloat16))
            if len(acts) == DOWN_GROUP or c + 1 == n_chunks:
                lo = (c + 1 - len(acts)) * ch
                act = jnp.concatenate(acts, axis=-1) if len(acts) > 1 else acts[0]
                acc = acc + jnp.dot(act, wdown_ref[lo:(c + 1) * ch, :], preferred_element_type=jnp.float32)
                acts = []
            up = up_next

        h2 = acc
        res = ((h2 * _rms_scale(h2)) * gfin_ref[...]).astype(o_ref.dtype)

        @pl.when(t > 0)
        def _():
            emit(res[0:N_META, :])

        prev_ref[...] = res


def _mix_out(h, attn, sga, pm, w_attn_br, w_out, g_ffn, w_up, conv_w, conv_b, w_down, g_final, seq):
    B, L, D = h.shape
    tm = TM_OUT
    n_tiles = L // tm
    assert D_FF % FF_CHUNK == 0 and L % tm == 0 and pl.cdiv(seq, tm) == n_tiles and tm > N_META
    row = pl.BlockSpec((None, tm, D), lambda b, t: (b, jnp.minimum(t, n_tiles - 1), 0))
    return pl.pallas_call(
        _mix_out_kernel,
        grid=(B, n_tiles + 1),
        in_specs=[
            row, row, row, row,
            _resident(w_attn_br.shape),
            _resident(w_out.shape),
            _resident((1, D)),
            _resident(w_up.shape),
            _resident(conv_w.shape),
            _resident(conv_b.shape),
            _resident(w_down.shape),
            _resident((1, D)),
        ],
        out_specs=pl.BlockSpec((None, tm, D), lambda b, t: (b, jnp.maximum(t - 1, 0), 0)),
        out_shape=jax.ShapeDtypeStruct((B, seq, D), jnp.float32),
        scratch_shapes=[pltpu.VMEM((CONV_HALO, 2 * D_FF), jnp.float32),
                        pltpu.VMEM((tm, D), jnp.float32)],
        compiler_params=pltpu.CompilerParams(
            dimension_semantics=("arbitrary", "arbitrary"), vmem_limit_bytes=VMEM_LIMIT),
        name="mix_out",
    )(h, attn, sga, pm, w_attn_br, w_out, g_ffn, w_up, conv_w, conv_b, w_down, g_final)


def _rope_tables(length):
    pos = np.arange(length, dtype=np.float32)
    inv = (1.0 / (np.float32(ROPE_THETA) ** (np.arange(0, HEAD_DIM, 2, dtype=np.float32) / np.float32(HEAD_DIM))))
    ang = pos[:, None] * inv.astype(np.float32)[None, :]
    cos, sin = np.cos(ang), np.sin(ang)
    cos_t = np.concatenate([cos, cos, cos, cos], axis=-1)
    sin_t = np.concatenate([-sin, sin, -sin, sin], axis=-1)
    return jnp.asarray(cos_t, jnp.float32), jnp.asarray(sin_t, jnp.float32)


def kernel(x, meta_tokens, g_mix, w_in, lam, g_subln, w_pool_grp, pool_scale, w_attn_br, w_pool_br,
           w_out, g_ffn, w_up, conv_w, conv_b, w_down, g_final):
    B, S, D = x.shape
    L = N_META + S
    L_pad = ((L + ROW_ALIGN - 1) // ROW_ALIGN) * ROW_ALIGN
    bf = jnp.bfloat16
    cos_t, sin_t = _rope_tables(L_pad)

    h, qq, k, vt, sga, pm = _mix_in(x, meta_tokens.astype(x.dtype), L_pad, g_mix[0][None], w_in[0].astype(bf),
                                    cos_t, sin_t, w_pool_grp[0].astype(bf), pool_scale[0][None],
                                    w_pool_br[0].astype(bf))
    attn = _diffattn(lam[0], g_subln[0][:, None], qq, k, vt, L)
    return _mix_out(h, attn, sga, pm, w_attn_br[0].astype(bf), w_out[0].astype(bf), g_ffn[0][None],
                    w_up[0].astype(bf), conv_w[0], conv_b[0][None], w_down[0].astype(bf), g_final[None], S)
```

```python
import functools
import math

import jax
import jax.numpy as jnp
import numpy as np
from jax import lax
from jax.experimental import pallas as pl
from jax.experimental.pallas import tpu as pltpu

D_MODEL = 1024
N_META = 16
N_HEADS = 8
HEAD_DIM = 64
V_DIM = 2 * HEAD_DIM
QK_WIDTH = N_HEADS * 2 * HEAD_DIM
ATTN_WIDTH = N_HEADS * V_DIM
POOL_GROUPS = 4
POOL_WINDOWS = (2, 4, 8, 16)
POOL_WIDTH = 512
POOL_GDIM = POOL_WIDTH // POOL_GROUPS
D_FF = 2816
CONV_WIDTH = 3
ROPE_THETA = 10000.0
EPS = 1e-6
LAM_INIT = 0.8 - 0.6 * math.exp(-0.3 * 0)
LOG2E = math.log2(math.e)

COL_Q = 0
COL_K = QK_WIDTH
COL_V = 2 * QK_WIDTH
COL_U = 2 * QK_WIDTH + ATTN_WIDTH
COL_GA = COL_U + POOL_WIDTH
COL_GP = COL_GA + D_MODEL

LANES = 128
MXU_TILE = 256
POOL_HALO = 16
CONV_HALO = 8
VMEM_LIMIT = 56 * 1024 * 1024

ROW_ALIGN = 768
TM_IN = 768
X_BLOCK = 256
TQ = 768
TK = MXU_TILE
SUM_ROWS = 16
TM_OUT = 704
FF_CHUNK = 256
DOWN_GROUP = 11


def _rms_scale(x):
    return lax.rsqrt(jnp.mean(x * x, axis=-1, keepdims=True) + EPS)


def _resident(shape):
    return pl.BlockSpec(shape, lambda *_: (0,) * len(shape), pipeline_mode=pl.Buffered(1))


def _mix_in_kernel(xf_ref, xa_ref, xb_ref, xc_ref, meta_ref, g_ref, w_in_ref, cos_ref, sin_ref,
                   wgrp_ref, pscale_ref, wpbr_ref,
                   h_ref, qq_ref, k_ref, vt_ref, sga_ref, pm_ref, ubuf_ref, *, seq_len):
    t = pl.program_id(1)
    tm = h_ref.shape[0]
    pos = t * tm + lax.broadcasted_iota(jnp.int32, (tm, 1), 0)

    h_ref[0:N_META, :] = jnp.where(t == 0, meta_ref[...], xf_ref[...])
    h_ref[N_META:N_META + X_BLOCK, :] = xa_ref[...]
    h_ref[N_META + X_BLOCK:N_META + 2 * X_BLOCK, :] = xb_ref[...]
    h_ref[N_META + 2 * X_BLOCK:tm, :] = xc_ref[0:tm - N_META - 2 * X_BLOCK, :]
    h = jnp.where(pos < seq_len, h_ref[...], 0.0)
    h_ref[...] = h
    hn = ((h * _rms_scale(h)) * g_ref[...]).astype(jnp.bfloat16)

    def proj(col, width):
        return jnp.dot(hn, w_in_ref[:, col:col + width], preferred_element_type=jnp.float32)

    cos = cos_ref[...]
    sin = sin_ref[...]
    lane = lax.broadcasted_iota(jnp.int32, (tm, LANES), 1)
    first_half = (lane % HEAD_DIM) < (HEAD_DIM // 2)

    def rope(pj, hh):
        xh = pj[:, hh * LANES:(hh + 1) * LANES]
        partner = jnp.where(first_half,
                            pltpu.roll(xh, LANES - HEAD_DIM // 2, axis=1),
                            pltpu.roll(xh, HEAD_DIM // 2, axis=1))
        return xh * cos + partner * sin

    @pl.when(t == 0)
    def _():
        ubuf_ref[0:POOL_HALO, :] = jnp.zeros((POOL_HALO, POOL_WIDTH), jnp.float32)

    ubuf_ref[POOL_HALO:POOL_HALO + tm, :] = proj(COL_U, POOL_WIDTH)

    pj = proj(COL_Q, QK_WIDTH)
    zeros = jnp.zeros((HEAD_DIM, LANES), qq_ref.dtype)
    for hh in range(N_HEADS):
        qt = (rope(pj, hh) * (LOG2E / math.sqrt(HEAD_DIM))).astype(qq_ref.dtype).T
        for blk in range(tm // LANES):
            src = slice(blk * LANES, (blk + 1) * LANES)
            c0 = slice(2 * blk * LANES, (2 * blk + 1) * LANES)
            c1 = slice((2 * blk + 1) * LANES, (2 * blk + 2) * LANES)
            qq_ref[hh, 0:HEAD_DIM, c0] = qt[0:HEAD_DIM, src]
            qq_ref[hh, HEAD_DIM:V_DIM, c0] = zeros
            qq_ref[hh, 0:HEAD_DIM, c1] = zeros
            qq_ref[hh, HEAD_DIM:V_DIM, c1] = qt[HEAD_DIM:V_DIM, src]

    mixed = []
    for g, w in enumerate(POOL_WINDOWS):
        cols = slice(g * POOL_GDIM, (g + 1) * POOL_GDIM)
        u_g = ubuf_ref[POOL_HALO:POOL_HALO + tm, cols]
        sums = u_g
        for j in range(1, w):
            sums = sums + ubuf_ref[POOL_HALO - j:POOL_HALO - j + tm, cols]
        count = jnp.minimum(pos + 1, w).astype(jnp.float32)
        pooled = (sums / count - u_g).astype(jnp.bfloat16)
        m_g = jnp.dot(pooled, wgrp_ref[g], preferred_element_type=jnp.float32)
        mixed.append((m_g * pscale_ref[:, cols]).astype(jnp.bfloat16))
    pool = jnp.concatenate(mixed, axis=-1)

    pj = proj(COL_K, QK_WIDTH)
    for hh in range(N_HEADS):
        k_ref[:, hh * LANES:(hh + 1) * LANES] = rope(pj, hh).astype(k_ref.dtype)
    pbr = jnp.dot(pool, wpbr_ref[...], preferred_element_type=jnp.float32)

    v = proj(COL_V, ATTN_WIDTH).astype(vt_ref.dtype)
    ones = jnp.ones((SUM_ROWS, TK), vt_ref.dtype)
    for hh in range(N_HEADS):
        for c in range(tm // TK):
            vt_ref[hh, c, 0:V_DIM, :] = v[c * TK:(c + 1) * TK, hh * LANES:(hh + 1) * LANES].T
            vt_ref[hh, c, V_DIM:V_DIM + SUM_ROWS, :] = ones
    pm_ref[...] = (jax.nn.sigmoid(proj(COL_GP, D_MODEL)) * pbr).astype(pm_ref.dtype)
    sga_ref[...] = jax.nn.sigmoid(proj(COL_GA, D_MODEL)).astype(sga_ref.dtype)

    ubuf_ref[0:POOL_HALO, :] = ubuf_ref[tm:tm + POOL_HALO, :]


def _mix_in(x, meta, L, g_mix, w_in, cos_t, sin_t, w_grp, pool_scale, w_pool_br):
    B, S, D = x.shape
    tm = TM_IN
    n_xblk = S // X_BLOCK
    assert L % tm == 0 and S % X_BLOCK == 0 and tm == 3 * X_BLOCK and X_BLOCK % N_META == 0
    assert tm % TK == 0 and tm % LANES == 0
    row = lambda width: pl.BlockSpec((None, tm, width), lambda b, t: (b, t, 0))
    x_head = pl.BlockSpec((None, N_META, D), lambda b, t: (b, jnp.maximum(t * (tm // N_META) - 1, 0), 0))
    x_body = lambda i: pl.BlockSpec((None, X_BLOCK, D),
                                    lambda b, t: (b, jnp.minimum(t * (tm // X_BLOCK) + i, n_xblk - 1), 0))
    bf = jnp.bfloat16
    return pl.pallas_call(
        functools.partial(_mix_in_kernel, seq_len=N_META + S),
        grid=(B, L // tm),
        in_specs=[
            x_head, x_body(0), x_body(1), x_body(2),
            _resident((N_META, D)),
            _resident((1, D)),
            _resident(w_in.shape),
            pl.BlockSpec((tm, LANES), lambda b, t: (t, 0)),
            pl.BlockSpec((tm, LANES), lambda b, t: (t, 0)),
            _resident(w_grp.shape),
            _resident((1, POOL_WIDTH)),
            _resident(w_pool_br.shape),
        ],
        out_specs=[
            row(D),
            pl.BlockSpec((None, N_HEADS, V_DIM, 2 * tm), lambda b, t: (b, 0, 0, t)),
            row(QK_WIDTH),
            pl.BlockSpec((None, N_HEADS, tm // TK, V_DIM + SUM_ROWS, TK), lambda b, t: (b, 0, t, 0, 0)),
            row(D),
            row(D),
        ],
        out_shape=[
            jax.ShapeDtypeStruct((B, L, D), jnp.float32),
            jax.ShapeDtypeStruct((B, N_HEADS, V_DIM, 2 * L), bf),
            jax.ShapeDtypeStruct((B, L, QK_WIDTH), bf),
            jax.ShapeDtypeStruct((B, N_HEADS, L // TK, V_DIM + SUM_ROWS, TK), bf),
            jax.ShapeDtypeStruct((B, L, D), bf),
            jax.ShapeDtypeStruct((B, L, D), bf),
        ],
        scratch_shapes=[pltpu.VMEM((tm + POOL_HALO, POOL_WIDTH), jnp.float32)],
        compiler_params=pltpu.CompilerParams(
            dimension_semantics=("arbitrary", "arbitrary"), vmem_limit_bytes=VMEM_LIMIT),
        name="mix_in",
    )(x, x, x, x, meta, g_mix, w_in, cos_t, sin_t, w_grp, pool_scale, w_pool_br)


def _diffattn_kernel(lam_ref, gsub_ref, qq_ref, k_ref, vt_ref, o_ref, s_ref, cmax_ref, m_ref, acc_ref,
                     *, last_tile_blocks):
    qi = pl.program_id(2)
    n_q = pl.num_programs(2)
    n_qblk = TQ // LANES
    diag_chunks = TQ // TK
    n_slots = diag_chunks

    def run(n_blk):
        hi = 2 * LANES * n_blk
        m_ref[...] = jnp.full(m_ref.shape, -jnp.inf, jnp.float32)
        acc_ref[...] = jnp.zeros(acc_ref.shape, jnp.float32)

        def scores(j, slot, diag=None):
            lo = 0 if diag is None else diag * 2 * TK
            kc = k_ref[pl.ds(pl.multiple_of(j * TK, TK), TK), :]
            s = jnp.dot(kc, qq_ref[:, lo:hi], preferred_element_type=jnp.float32)
            if diag is not None:
                part = min(2 * TK, hi - lo)
                key = lax.broadcasted_iota(jnp.int32, (TK, part), 0)
                lane = lax.broadcasted_iota(jnp.int32, (TK, part), 1)
                visible = key <= (lane // (2 * LANES)) * LANES + lane % LANES
                masked = jnp.where(visible, s[:, 0:part], -jnp.inf)
                s = jnp.concatenate([masked, s[:, part:]], axis=1) if hi - lo > part else masked
            s_ref[slot, :, lo:hi] = s
            cmax_ref[slot, :, lo:hi] = jnp.max(s, axis=0, keepdims=True)

        def consume(j, slot, lo=0):
            m = m_ref[:, lo:hi]
            m_new = jnp.maximum(m, cmax_ref[slot, :, lo:hi])
            alpha = jnp.exp2(m - m_new)
            p = jnp.exp2(s_ref[slot, :, lo:hi] - m_new)
            pv = jnp.dot(vt_ref[j], p.astype(jnp.bfloat16), preferred_element_type=jnp.float32)
            acc_ref[:, lo:hi] = alpha * acc_ref[:, lo:hi] + pv
            m_ref[:, lo:hi] = m_new

        scores(0, 0)

        def block(base):
            for r in range(n_slots):
                scores(base + r + 1, (r + 1) % n_slots)
                consume(base + r, r)

        def trio(i, c):
            for b in range(3):
                block((3 * i + b) * n_slots)
            return c

        lax.fori_loop(0, qi // 3, trio, 0)

        @pl.when(qi % 3 == 1)
        def _():
            block((qi - 1) * n_slots)

        @pl.when(qi % 3 == 2)
        def _():
            block((qi - 2) * n_slots)
            block((qi - 1) * n_slots)

        base = qi * n_slots
        live = [d for d in range(diag_chunks) if d * 2 * TK < hi]
        scores(base, 0, diag=0)
        for d in live:
            if d + 1 in live:
                scores(base + d + 1, d + 1, diag=d + 1)
            consume(base + d, d, lo=d * 2 * TK)

        lam = lam_ref[...]
        lam_full = (jnp.exp(jnp.sum(lam[0:1, :] * lam[1:2, :], axis=-1, keepdims=True))
                    - jnp.exp(jnp.sum(lam[2:3, :] * lam[3:4, :], axis=-1, keepdims=True)) + LAM_INIT)
        gsub = gsub_ref[...] * (1.0 - LAM_INIT)
        for blk in range(n_qblk):
            rows = slice(blk * LANES, (blk + 1) * LANES)
            if blk >= n_blk:
                o_ref[rows, :] = jnp.zeros((LANES, V_DIM), o_ref.dtype)
                continue
            c0 = slice(2 * blk * LANES, (2 * blk + 1) * LANES)
            c1 = slice((2 * blk + 1) * LANES, (2 * blk + 2) * LANES)
            o = (acc_ref[0:V_DIM, c0] / acc_ref[V_DIM:V_DIM + 1, c0]
                 - lam_full * (acc_ref[0:V_DIM, c1] / acc_ref[V_DIM:V_DIM + 1, c1]))
            o = o * lax.rsqrt(jnp.mean(o * o, axis=0, keepdims=True) + EPS) * gsub
            o_ref[rows, :] = o.T.astype(o_ref.dtype)

    if last_tile_blocks == n_qblk:
        run(n_qblk)
    else:
        pl.when(qi < n_q - 1)(lambda: run(n_qblk))
        pl.when(qi == n_q - 1)(lambda: run(last_tile_blocks))


def _diffattn(lam, g_subln, qq, k, vt, seq_len):
    B, L, _ = k.shape
    assert TQ % TK == 0 and L % TQ == 0 and TQ % LANES == 0 and L - TQ < seq_len <= L
    last_tile_blocks = pl.cdiv(seq_len - (L - TQ), LANES)
    return pl.pallas_call(
        functools.partial(_diffattn_kernel, last_tile_blocks=last_tile_blocks),
        grid=(B, N_HEADS, L // TQ),
        in_specs=[
            _resident(lam.shape),
            _resident((V_DIM, 1)),
            pl.BlockSpec((None, None, V_DIM, 2 * TQ), lambda b, h, i: (b, h, 0, i)),
            pl.BlockSpec((None, L, V_DIM), lambda b, h, i: (b, 0, h)),
            pl.BlockSpec((None, None, L // TK, V_DIM + SUM_ROWS, TK), lambda b, h, i: (b, h, 0, 0, 0)),
        ],
        out_specs=pl.BlockSpec((None, TQ, V_DIM), lambda b, h, i: (b, i, h)),
        out_shape=jax.ShapeDtypeStruct((B, L, ATTN_WIDTH), jnp.bfloat16),
        scratch_shapes=[pltpu.VMEM((TQ // TK, TK, 2 * TQ), jnp.float32),
                        pltpu.VMEM((TQ // TK, 1, 2 * TQ), jnp.float32),
                        pltpu.VMEM((1, 2 * TQ), jnp.float32),
                        pltpu.VMEM((V_DIM + SUM_ROWS, 2 * TQ), jnp.float32)],
        compiler_params=pltpu.CompilerParams(
            dimension_semantics=("arbitrary", "arbitrary", "arbitrary"), vmem_limit_bytes=VMEM_LIMIT),
        name="diffattn",
    )(lam, g_subln, qq, k, vt)


def _mix_out_kernel(h_ref, attn_ref, sga_ref, pm_ref, wabr_ref, wout_ref, gffn_ref, wup_ref,
                    convw_ref, convb_ref, wdown_ref, gfin_ref, o_ref, halo_ref, prev_ref):
    t = pl.program_id(1)
    n_tiles = pl.num_programs(1) - 1
    tm = h_ref.shape[0]
    ch = FF_CHUNK

    def emit(first_rows_of_next):
        o_ref[0:tm - N_META, :] = prev_ref[N_META:tm, :]
        o_ref[tm - N_META:tm, :] = first_rows_of_next

    @pl.when(t == n_tiles)
    def _():
        emit(jnp.zeros((N_META, D_MODEL), o_ref.dtype))

    @pl.when(t < n_tiles)
    def _():
        abr = jnp.dot(attn_ref[...], wabr_ref[...], preferred_element_type=jnp.float32)
        merged = sga_ref[...].astype(jnp.float32) * abr + pm_ref[...].astype(jnp.float32)
        h1 = h_ref[...] + jnp.dot(merged.astype(jnp.bfloat16), wout_ref[...],
                                  preferred_element_type=jnp.float32)
        hn = ((h1 * _rms_scale(h1)) * gffn_ref[...]).astype(jnp.bfloat16)

        @pl.when(t == 0)
        def _():
            halo_ref[...] = jnp.zeros_like(halo_ref)

        def up_proj(c):
            return tuple(jnp.dot(hn, wup_ref[:, col:col + ch], preferred_element_type=jnp.float32)
                         for col in (c * ch, D_FF + c * ch))

        def conv(up, col):
            taps = [convw_ref[CONV_WIDTH - 1 - j:CONV_WIDTH - j, col:col + ch] for j in range(CONV_WIDTH)]
            bias = convb_ref[:, col:col + ch]
            y = bias + taps[0] * up
            for j in range(1, CONV_WIDTH):
                y = y + taps[j] * pltpu.roll(up, j, axis=0)
            ext = jnp.concatenate([halo_ref[:, col:col + ch], up[0:CONV_HALO, :]], axis=0)
            y0 = bias + taps[0] * ext[CONV_HALO:2 * CONV_HALO, :]
            for j in range(1, CONV_WIDTH):
                y0 = y0 + taps[j] * ext[CONV_HALO - j:2 * CONV_HALO - j, :]
            halo_ref[:, col:col + ch] = up[tm - CONV_HALO:tm, :]
            return jnp.concatenate([y0, y[CONV_HALO:, :]], axis=0)

        n_chunks = D_FF // ch
        acc = h1
        up = up_proj(0)
        acts = []
        for c in range(n_chunks):
            up_next = up_proj(c + 1) if c + 1 < n_chunks else None
            val = conv(up[0], c * ch)
            gate = conv(up[1], D_FF + c * ch)
            acts.append((jax.nn.silu(gate) * val).astype(jnp.bfloat16))
            if len(acts) == DOWN_GROUP or c + 1 == n_chunks:
                lo = (c + 1 - len(acts)) * ch
                act = jnp.concatenate(acts, axis=-1) if len(acts) > 1 else acts[0]
                acc = acc + jnp.dot(act, wdown_ref[lo:(c + 1) * ch, :], preferred_element_type=jnp.float32)
                acts = []
            up = up_next

        h2 = acc
        res = ((h2 * _rms_scale(h2)) * gfin_ref[...]).astype(o_ref.dtype)

        @pl.when(t > 0)
        def _():
            emit(res[0:N_META, :])

        prev_ref[...] = res


def _mix_out(h, attn, sga, pm, w_attn_br, w_out, g_ffn, w_up, conv_w, conv_b, w_down, g_final, seq):
    B, L, D = h.shape
    tm = TM_OUT
    n_tiles = L // tm
    assert D_FF % FF_CHUNK == 0 and L % tm == 0 and pl.cdiv(seq, tm) == n_tiles and tm > N_META
    row = pl.BlockSpec((None, tm, D), lambda b, t: (b, jnp.minimum(t, n_tiles - 1), 0))
    return pl.pallas_call(
        _mix_out_kernel,
        grid=(B, n_tiles + 1),
        in_specs=[
            row, row, row, row,
            _resident(w_attn_br.shape),
            _resident(w_out.shape),
            _resident((1, D)),
            _resident(w_up.shape),
            _resident(conv_w.shape),
            _resident(conv_b.shape),
            _resident(w_down.shape),
            _resident((1, D)),
        ],
        out_specs=pl.BlockSpec((None, tm, D), lambda b, t: (b, jnp.maximum(t - 1, 0), 0)),
        out_shape=jax.ShapeDtypeStruct((B, seq, D), jnp.float32),
        scratch_shapes=[pltpu.VMEM((CONV_HALO, 2 * D_FF), jnp.float32),
                        pltpu.VMEM((tm, D), jnp.float32)],
        compiler_params=pltpu.CompilerParams(
            dimension_semantics=("arbitrary", "arbitrary"), vmem_limit_bytes=VMEM_LIMIT),
        name="mix_out",
    )(h, attn, sga, pm, w_attn_br, w_out, g_ffn, w_up, conv_w, conv_b, w_down, g_final)


def _rope_tables(length):
    pos = np.arange(length, dtype=np.float32)
    inv = (1.0 / (np.float32(ROPE_THETA) ** (np.arange(0, HEAD_DIM, 2, dtype=np.float32) / np.float32(HEAD_DIM))))
    ang = pos[:, None] * inv.astype(np.float32)[None, :]
    cos, sin = np.cos(ang), np.sin(ang)
    cos_t = np.concatenate([cos, cos, cos, cos], axis=-1)
    sin_t = np.concatenate([-sin, sin, -sin, sin], axis=-1)
    return jnp.asarray(cos_t, jnp.float32), jnp.asarray(sin_t, jnp.float32)


def kernel(x, meta_tokens, g_mix, w_in, lam, g_subln, w_pool_grp, pool_scale, w_attn_br, w_pool_br,
           w_out, g_ffn, w_up, conv_w, conv_b, w_down, g_final):
    B, S, D = x.shape
    L = N_META + S
    L_pad = ((L + ROW_ALIGN - 1) // ROW_ALIGN) * ROW_ALIGN
    bf = jnp.bfloat16
    cos_t, sin_t = _rope_tables(L_pad)

    h, qq, k, vt, sga, pm = _mix_in(x, meta_tokens.astype(x.dtype), L_pad, g_mix[0][None], w_in[0].astype(bf),
                                    cos_t, sin_t, w_pool_grp[0].astype(bf), pool_scale[0][None],
                                    w_pool_br[0].astype(bf))
    attn = _diffattn(lam[0], g_subln[0][:, None], qq, k, vt, L)
    return _mix_out(h, attn, sga, pm, w_attn_br[0].astype(bf), w_out[0].astype(bf), g_ffn[0][None],
                    w_up[0].astype(bf), conv_w[0], conv_b[0][None], w_down[0].astype(bf), g_final[None], S)
```

```python
import functools
import math

import jax
import jax.numpy as jnp
import numpy as np
from jax import lax
from jax.experimental import pallas as pl
from jax.experimental.pallas import tpu as pltpu

D_MODEL = 1024
N_META = 16
N_HEADS = 8
HEAD_DIM = 64
V_DIM = 2 * HEAD_DIM
QK_WIDTH = N_HEADS * 2 * HEAD_DIM
ATTN_WIDTH = N_HEADS * V_DIM
POOL_GROUPS = 4
POOL_WINDOWS = (2, 4, 8, 16)
POOL_WIDTH = 512
POOL_GDIM = POOL_WIDTH // POOL_GROUPS
D_FF = 2816
CONV_WIDTH = 3
ROPE_THETA = 10000.0
EPS = 1e-6
LAM_INIT = 0.8 - 0.6 * math.exp(-0.3 * 0)
LOG2E = math.log2(math.e)

COL_Q = 0
COL_K = QK_WIDTH
COL_V = 2 * QK_WIDTH
COL_U = 2 * QK_WIDTH + ATTN_WIDTH
COL_GA = COL_U + POOL_WIDTH
COL_GP = COL_GA + D_MODEL

LANES = 128
MXU_TILE = 256
POOL_HALO = 16
CONV_HALO = 8
VMEM_LIMIT = 56 * 1024 * 1024

ROW_ALIGN = 768
TM_IN = 768
X_BLOCK = 256
TQ = 768
TK = MXU_TILE
SUM_ROWS = 16
TM_OUT = 704
FF_CHUNK = 256
DOWN_GROUP = 11


def _rms_scale(x):
    return lax.rsqrt(jnp.mean(x * x, axis=-1, keepdims=True) + EPS)


def _resident(shape):
    return pl.BlockSpec(shape, lambda *_: (0,) * len(shape), pipeline_mode=pl.Buffered(1))


def _mix_in_kernel(xf_ref, xa_ref, xb_ref, xc_ref, meta_ref, g_ref, w_in_ref, cos_ref, sin_ref,
                   wgrp_ref, pscale_ref, wpbr_ref,
                   h_ref, qq_ref, k_ref, vt_ref, sga_ref, pm_ref, ubuf_ref, *, seq_len):
    t = pl.program_id(1)
    tm = h_ref.shape[0]
    pos = t * tm + lax.broadcasted_iota(jnp.int32, (tm, 1), 0)

    h_ref[0:N_META, :] = jnp.where(t == 0, meta_ref[...], xf_ref[...])
    h_ref[N_META:N_META + X_BLOCK, :] = xa_ref[...]
    h_ref[N_META + X_BLOCK:N_META + 2 * X_BLOCK, :] = xb_ref[...]
    h_ref[N_META + 2 * X_BLOCK:tm, :] = xc_ref[0:tm - N_META - 2 * X_BLOCK, :]
    h = jnp.where(pos < seq_len, h_ref[...], 0.0)
    h_ref[...] = h
    hn = ((h * _rms_scale(h)) * g_ref[...]).astype(jnp.bfloat16)

    def proj(col, width):
        return jnp.dot(hn, w_in_ref[:, col:col + width], preferred_element_type=jnp.float32)

    cos = cos_ref[...]
    sin = sin_ref[...]
    lane = lax.broadcasted_iota(jnp.int32, (tm, LANES), 1)
    first_half = (lane % HEAD_DIM) < (HEAD_DIM // 2)

    def rope(pj, hh):
        xh = pj[:, hh * LANES:(hh + 1) * LANES]
        partner = jnp.where(first_half,
                            pltpu.roll(xh, LANES - HEAD_DIM // 2, axis=1),
                            pltpu.roll(xh, HEAD_DIM // 2, axis=1))
        return xh * cos + partner * sin

    @pl.when(t == 0)
    def _():
        ubuf_ref[0:POOL_HALO, :] = jnp.zeros((POOL_HALO, POOL_WIDTH), jnp.float32)

    ubuf_ref[POOL_HALO:POOL_HALO + tm, :] = proj(COL_U, POOL_WIDTH)

    pj = proj(COL_Q, QK_WIDTH)
    zeros = jnp.zeros((HEAD_DIM, LANES), qq_ref.dtype)
    for hh in range(N_HEADS):
        qt = (rope(pj, hh) * (LOG2E / math.sqrt(HEAD_DIM))).astype(qq_ref.dtype).T
        for blk in range(tm // LANES):
            src = slice(blk * LANES, (blk + 1) * LANES)
            c0 = slice(2 * blk * LANES, (2 * blk + 1) * LANES)
            c1 = slice((2 * blk + 1) * LANES, (2 * blk + 2) * LANES)
            qq_ref[hh, 0:HEAD_DIM, c0] = qt[0:HEAD_DIM, src]
            qq_ref[hh, HEAD_DIM:V_DIM, c0] = zeros
            qq_ref[hh, 0:HEAD_DIM, c1] = zeros
            qq_ref[hh, HEAD_DIM:V_DIM, c1] = qt[HEAD_DIM:V_DIM, src]

    mixed = []
    for g, w in enumerate(POOL_WINDOWS):
        cols = slice(g * POOL_GDIM, (g + 1) * POOL_GDIM)
        u_g = ubuf_ref[POOL_HALO:POOL_HALO + tm, cols]
        sums = u_g
        for j in range(1, w):
            sums = sums + ubuf_ref[POOL_HALO - j:POOL_HALO - j + tm, cols]
        count = jnp.minimum(pos + 1, w).astype(jnp.float32)
        pooled = (sums / count - u_g).astype(jnp.bfloat16)
        m_g = jnp.dot(pooled, wgrp_ref[g], preferred_element_type=jnp.float32)
        mixed.append((m_g * pscale_ref[:, cols]).astype(jnp.bfloat16))
    pool = jnp.concatenate(mixed, axis=-1)

    pj = proj(COL_K, QK_WIDTH)
    for hh in range(N_HEADS):
        k_ref[:, hh * LANES:(hh + 1) * LANES] = rope(pj, hh).astype(k_ref.dtype)
    pbr = jnp.dot(pool, wpbr_ref[...], preferred_element_type=jnp.float32)
    pm_ref[...] = (jax.nn.sigmoid(proj(COL_GP, D_MODEL)) * pbr).astype(pm_ref.dtype)
    sga_ref[...] = jax.nn.sigmoid(proj(COL_GA, D_MODEL)).astype(sga_ref.dtype)

    v = proj(COL_V, ATTN_WIDTH).astype(vt_ref.dtype)
    ones = jnp.ones((SUM_ROWS, TK), vt_ref.dtype)
    for hh in range(N_HEADS):
        for c in range(tm // TK):
            vt_ref[hh, c, 0:V_DIM, :] = v[c * TK:(c + 1) * TK, hh * LANES:(hh + 1) * LANES].T
            vt_ref[hh, c, V_DIM:V_DIM + SUM_ROWS, :] = ones

    ubuf_ref[0:POOL_HALO, :] = ubuf_ref[tm:tm + POOL_HALO, :]


def _mix_in(x, meta, L, g_mix, w_in, cos_t, sin_t, w_grp, pool_scale, w_pool_br):
    B, S, D = x.shape
    tm = TM_IN
    n_xblk = S // X_BLOCK
    assert L % tm == 0 and S % X_BLOCK == 0 and tm == 3 * X_BLOCK and X_BLOCK % N_META == 0
    assert tm % TK == 0 and tm % LANES == 0
    row = lambda width: pl.BlockSpec((None, tm, width), lambda b, t: (b, t, 0))
    x_head = pl.BlockSpec((None, N_META, D), lambda b, t: (b, jnp.maximum(t * (tm // N_META) - 1, 0), 0))
    x_body = lambda i: pl.BlockSpec((None, X_BLOCK, D),
                                    lambda b, t: (b, jnp.minimum(t * (tm // X_BLOCK) + i, n_xblk - 1), 0))
    bf = jnp.bfloat16
    return pl.pallas_call(
        functools.partial(_mix_in_kernel, seq_len=N_META + S),
        grid=(B, L // tm),
        in_specs=[
            x_head, x_body(0), x_body(1), x_body(2),
            _resident((N_META, D)),
            _resident((1, D)),
            _resident(w_in.shape),
            pl.BlockSpec((tm, LANES), lambda b, t: (t, 0)),
            pl.BlockSpec((tm, LANES), lambda b, t: (t, 0)),
            _resident(w_grp.shape),
            _resident((1, POOL_WIDTH)),
            _resident(w_pool_br.shape),
        ],
        out_specs=[
            row(D),
            pl.BlockSpec((None, N_HEADS, V_DIM, 2 * tm), lambda b, t: (b, 0, 0, t)),
            row(QK_WIDTH),
            pl.BlockSpec((None, N_HEADS, tm // TK, V_DIM + SUM_ROWS, TK), lambda b, t: (b, 0, t, 0, 0)),
            row(D),
            row(D),
        ],
        out_shape=[
            jax.ShapeDtypeStruct((B, L, D), jnp.float32),
            jax.ShapeDtypeStruct((B, N_HEADS, V_DIM, 2 * L), bf),
            jax.ShapeDtypeStruct((B, L, QK_WIDTH), bf),
            jax.ShapeDtypeStruct((B, N_HEADS, L // TK, V_DIM + SUM_ROWS, TK), bf),
            jax.ShapeDtypeStruct((B, L, D), bf),
            jax.ShapeDtypeStruct((B, L, D), bf),
        ],
        scratch_shapes=[pltpu.VMEM((tm + POOL_HALO, POOL_WIDTH), jnp.float32)],
        compiler_params=pltpu.CompilerParams(
            dimension_semantics=("arbitrary", "arbitrary"), vmem_limit_bytes=VMEM_LIMIT),
        name="mix_in",
    )(x, x, x, x, meta, g_mix, w_in, cos_t, sin_t, w_grp, pool_scale, w_pool_br)


def _diffattn_kernel(lam_ref, gsub_ref, qq_ref, k_ref, vt_ref, o_ref, s_ref, cmax_ref, m_ref, acc_ref,
                     *, last_tile_blocks):
    qi = pl.program_id(2)
    n_q = pl.num_programs(2)
    n_qblk = TQ // LANES
    diag_chunks = TQ // TK
    n_slots = diag_chunks

    def run(n_blk):
        hi = 2 * LANES * n_blk
        m_ref[...] = jnp.full(m_ref.shape, -jnp.inf, jnp.float32)
        acc_ref[...] = jnp.zeros(acc_ref.shape, jnp.float32)

        def scores(j, slot, diag=None):
            lo = 0 if diag is None else diag * 2 * TK
            kc = k_ref[pl.ds(pl.multiple_of(j * TK, TK), TK), :]
            s = jnp.dot(kc, qq_ref[:, lo:hi], preferred_element_type=jnp.float32)
            if diag is not None:
                part = min(2 * TK, hi - lo)
                key = lax.broadcasted_iota(jnp.int32, (TK, part), 0)
                lane = lax.broadcasted_iota(jnp.int32, (TK, part), 1)
                visible = key <= (lane // (2 * LANES)) * LANES + lane % LANES
                masked = jnp.where(visible, s[:, 0:part], -jnp.inf)
                s = jnp.concatenate([masked, s[:, part:]], axis=1) if hi - lo > part else masked
            s_ref[slot, :, lo:hi] = s
            cmax_ref[slot, :, lo:hi] = jnp.max(s, axis=0, keepdims=True)

        def consume(j, slot, lo=0):
            m = m_ref[:, lo:hi]
            m_new = jnp.maximum(m, cmax_ref[slot, :, lo:hi])
            alpha = jnp.exp2(m - m_new)
            p = jnp.exp2(s_ref[slot, :, lo:hi] - m_new)
            pv = jnp.dot(vt_ref[j], p.astype(jnp.bfloat16), preferred_element_type=jnp.float32)
            acc_ref[:, lo:hi] = alpha * acc_ref[:, lo:hi] + pv
            m_ref[:, lo:hi] = m_new

        scores(0, 0)

        def block(base):
            for r in range(n_slots):
                scores(base + r + 1, (r + 1) % n_slots)
                consume(base + r, r)

        def quad(i, c):
            for b in range(4):
                block((4 * i + b) * n_slots)
            return c

        lax.fori_loop(0, qi // 4, quad, 0)

        for rem in (1, 2, 3):
            @pl.when(qi % 4 == rem)
            def _(rem=rem):
                for b in range(rem):
                    block((qi - rem + b) * n_slots)

        base = qi * n_slots
        live = [d for d in range(diag_chunks) if d * 2 * TK < hi]
        scores(base, 0, diag=0)
        for d in live:
            if d + 1 in live:
                scores(base + d + 1, d + 1, diag=d + 1)
            consume(base + d, d, lo=d * 2 * TK)

        lam = lam_ref[...]
        lam_full = (jnp.exp(jnp.sum(lam[0:1, :] * lam[1:2, :], axis=-1, keepdims=True))
                    - jnp.exp(jnp.sum(lam[2:3, :] * lam[3:4, :], axis=-1, keepdims=True)) + LAM_INIT)
        gsub = gsub_ref[...] * (1.0 - LAM_INIT)
        for blk in range(n_qblk):
            rows = slice(blk * LANES, (blk + 1) * LANES)
            if blk >= n_blk:
                o_ref[rows, :] = jnp.zeros((LANES, V_DIM), o_ref.dtype)
                continue
            c0 = slice(2 * blk * LANES, (2 * blk + 1) * LANES)
            c1 = slice((2 * blk + 1) * LANES, (2 * blk + 2) * LANES)
            o = (acc_ref[0:V_DIM, c0] / acc_ref[V_DIM:V_DIM + 1, c0]
                 - lam_full * (acc_ref[0:V_DIM, c1] / acc_ref[V_DIM:V_DIM + 1, c1]))
            o = o * lax.rsqrt(jnp.mean(o * o, axis=0, keepdims=True) + EPS) * gsub
            o_ref[rows, :] = o.T.astype(o_ref.dtype)

    if last_tile_blocks == n_qblk:
        run(n_qblk)
    else:
        pl.when(qi < n_q - 1)(lambda: run(n_qblk))
        pl.when(qi == n_q - 1)(lambda: run(last_tile_blocks))


def _diffattn(lam, g_subln, qq, k, vt, seq_len):
    B, L, _ = k.shape
    assert TQ % TK == 0 and L % TQ == 0 and TQ % LANES == 0 and L - TQ < seq_len <= L
    last_tile_blocks = pl.cdiv(seq_len - (L - TQ), LANES)
    return pl.pallas_call(
        functools.partial(_diffattn_kernel, last_tile_blocks=last_tile_blocks),
        grid=(B, N_HEADS, L // TQ),
        in_specs=[
            _resident(lam.shape),
            _resident((V_DIM, 1)),
            pl.BlockSpec((None, None, V_DIM, 2 * TQ), lambda b, h, i: (b, h, 0, i)),
            pl.BlockSpec((None, L, V_DIM), lambda b, h, i: (b, 0, h)),
            pl.BlockSpec((None, None, L // TK, V_DIM + SUM_ROWS, TK), lambda b, h, i: (b, h, 0, 0, 0)),
        ],
        out_specs=pl.BlockSpec((None, TQ, V_DIM), lambda b, h, i: (b, i, h)),
        out_shape=jax.ShapeDtypeStruct((B, L, ATTN_WIDTH), jnp.bfloat16),
        scratch_shapes=[pltpu.VMEM((TQ // TK, TK, 2 * TQ), jnp.float32),
                        pltpu.VMEM((TQ // TK, 1, 2 * TQ), jnp.float32),
                        pltpu.VMEM((1, 2 * TQ), jnp.float32),
                        pltpu.VMEM((V_DIM + SUM_ROWS, 2 * TQ), jnp.float32)],
        compiler_params=pltpu.CompilerParams(
            dimension_semantics=("arbitrary", "arbitrary", "arbitrary"), vmem_limit_bytes=VMEM_LIMIT),
        name="diffattn",
    )(lam, g_subln, qq, k, vt)


def _mix_out_kernel(h_ref, attn_ref, sga_ref, pm_ref, wabr_ref, wout_ref, gffn_ref, wup_ref,
                    convw_ref, convb_ref, wdown_ref, gfin_ref, o_ref, halo_ref, prev_ref):
    t = pl.program_id(1)
    n_tiles = pl.num_programs(1) - 1
    tm = h_ref.shape[0]
    ch = FF_CHUNK

    def emit(first_rows_of_next):
        o_ref[0:tm - N_META, :] = prev_ref[N_META:tm, :]
        o_ref[tm - N_META:tm, :] = first_rows_of_next

    @pl.when(t == n_tiles)
    def _():
        emit(jnp.zeros((N_META, D_MODEL), o_ref.dtype))

    @pl.when(t < n_tiles)
    def _():
        abr = jnp.dot(attn_ref[...], wabr_ref[...], preferred_element_type=jnp.float32)
        merged = sga_ref[...].astype(jnp.float32) * abr + pm_ref[...].astype(jnp.float32)
        h1 = h_ref[...] + jnp.dot(merged.astype(jnp.bfloat16), wout_ref[...],
                                  preferred_element_type=jnp.float32)
        hn = ((h1 * _rms_scale(h1)) * gffn_ref[...]).astype(jnp.bfloat16)

        @pl.when(t == 0)
        def _():
            halo_ref[...] = jnp.zeros_like(halo_ref)

        def up_proj(c):
            return tuple(jnp.dot(hn, wup_ref[:, col:col + ch], preferred_element_type=jnp.float32)
                         for col in (c * ch, D_FF + c * ch))

        def conv(up, col):
            taps = [convw_ref[CONV_WIDTH - 1 - j:CONV_WIDTH - j, col:col + ch] for j in range(CONV_WIDTH)]
            bias = convb_ref[:, col:col + ch]
            y = bias + taps[0] * up
            for j in range(1, CONV_WIDTH):
                y = y + taps[j] * pltpu.roll(up, j, axis=0)
            ext = jnp.concatenate([halo_ref[:, col:col + ch], up[0:CONV_HALO, :]], axis=0)
            y0 = bias + taps[0] * ext[CONV_HALO:2 * CONV_HALO, :]
            for j in range(1, CONV_WIDTH):
                y0 = y0 + taps[j] * ext[CONV_HALO - j:2 * CONV_HALO - j, :]
            halo_ref[:, col:col + ch] = up[tm - CONV_HALO:tm, :]
            return jnp.concatenate([y0, y[CONV_HALO:, :]], axis=0)

        n_chunks = D_FF // ch
        acc = h1
        up = up_proj(0)
        acts = []
        for c in range(n_chunks):
            up_next = up_proj(c + 1) if c + 1 < n_chunks else None
            val = conv(up[0], c * ch)
            gate = conv(up[1], D_FF + c * ch)
            acts.append((jax.nn.silu(gate) * val).astype(jnp.bfloat16))
            if len(acts) == DOWN_GROUP or c + 1 == n_chunks:
                lo = (c + 1 - len(acts)) * ch
                act = jnp.concatenate(acts, axis=-1) if len(acts) > 1 else acts[0]
                acc = jnp.concatenate(
                    [acc[r, :] + jnp.dot(act[r, :], wdown_ref[lo:(c + 1) * ch, :], preferred_element_type=jnp.float32)
                     for r in (slice(0, tm // 2), slice(tm // 2, tm))], axis=0)
                acts = []
            up = up_next

        h2 = acc
        res = ((h2 * _rms_scale(h2)) * gfin_ref[...]).astype(o_ref.dtype)

        @pl.when(t > 0)
        def _():
            emit(res[0:N_META, :])

        prev_ref[...] = res


def _mix_out(h, attn, sga, pm, w_attn_br, w_out, g_ffn, w_up, conv_w, conv_b, w_down, g_final, seq):
    B, L, D = h.shape
    tm = TM_OUT
    n_tiles = L // tm
    assert D_FF % FF_CHUNK == 0 and L % tm == 0 and pl.cdiv(seq, tm) == n_tiles and tm > N_META
    row = pl.BlockSpec((None, tm, D), lambda b, t: (b, jnp.minimum(t, n_tiles - 1), 0))
    return pl.pallas_call(
        _mix_out_kernel,
        grid=(B, n_tiles + 1),
        in_specs=[
            row, row, row, row,
            _resident(w_attn_br.shape),
            _resident(w_out.shape),
            _resident((1, D)),
            _resident(w_up.shape),
            _resident(conv_w.shape),
            _resident(conv_b.shape),
            _resident(w_down.shape),
            _resident((1, D)),
        ],
        out_specs=pl.BlockSpec((None, tm, D), lambda b, t: (b, jnp.maximum(t - 1, 0), 0)),
        out_shape=jax.ShapeDtypeStruct((B, seq, D), jnp.float32),
        scratch_shapes=[pltpu.VMEM((CONV_HALO, 2 * D_FF), jnp.float32),
                        pltpu.VMEM((tm, D), jnp.float32)],
        compiler_params=pltpu.CompilerParams(
            dimension_semantics=("arbitrary", "arbitrary"), vmem_limit_bytes=VMEM_LIMIT),
        name="mix_out",
    )(h, attn, sga, pm, w_attn_br, w_out, g_ffn, w_up, conv_w, conv_b, w_down, g_final)


def _rope_tables(length):
    pos = np.arange(length, dtype=np.float32)
    inv = (1.0 / (np.float32(ROPE_THETA) ** (np.arange(0, HEAD_DIM, 2, dtype=np.float32) / np.float32(HEAD_DIM))))
    ang = pos[:, None] * inv.astype(np.float32)[None, :]
    cos, sin = np.cos(ang), np.sin(ang)
    cos_t = np.concatenate([cos, cos, cos, cos], axis=-1)
    sin_t = np.concatenate([-sin, sin, -sin, sin], axis=-1)
    return jnp.asarray(cos_t, jnp.float32), jnp.asarray(sin_t, jnp.float32)


def kernel(x, meta_tokens, g_mix, w_in, lam, g_subln, w_pool_grp, pool_scale, w_attn_br, w_pool_br,
           w_out, g_ffn, w_up, conv_w, conv_b, w_down, g_final):
    B, S, D = x.shape
    L = N_META + S
    L_pad = ((L + ROW_ALIGN - 1) // ROW_ALIGN) * ROW_ALIGN
    bf = jnp.bfloat16
    cos_t, sin_t = _rope_tables(L_pad)

    h, qq, k, vt, sga, pm = _mix_in(x, meta_tokens.astype(x.dtype), L_pad, g_mix[0][None], w_in[0].astype(bf),
                                    cos_t, sin_t, w_pool_grp[0].astype(bf), pool_scale[0][None],
                                    w_pool_br[0].astype(bf))
    attn = _diffattn(lam[0], g_subln[0][:, None], qq, k, vt, L)
    return _mix_out(h, attn, sga, pm, w_attn_br[0].astype(bf), w_out[0].astype(bf), g_ffn[0][None],
                    w_up[0].astype(bf), conv_w[0], conv_b[0][None], w_down[0].astype(bf), g_final[None], S)
```

```python
import functools
import math

import jax
import jax.numpy as jnp
import numpy as np
from jax import lax
from jax.experimental import pallas as pl
from jax.experimental.pallas import tpu as pltpu

D_MODEL = 1024
N_META = 16
N_HEADS = 8
HEAD_DIM = 64
V_DIM = 2 * HEAD_DIM
QK_WIDTH = N_HEADS * 2 * HEAD_DIM
ATTN_WIDTH = N_HEADS * V_DIM
POOL_GROUPS = 4
POOL_WINDOWS = (2, 4, 8, 16)
POOL_WIDTH = 512
POOL_GDIM = POOL_WIDTH // POOL_GROUPS
D_FF = 2816
CONV_WIDTH = 3
ROPE_THETA = 10000.0
EPS = 1e-6
LAM_INIT = 0.8 - 0.6 * math.exp(-0.3 * 0)
LOG2E = math.log2(math.e)

COL_Q = 0
COL_K = QK_WIDTH
COL_V = 2 * QK_WIDTH
COL_U = 2 * QK_WIDTH + ATTN_WIDTH
COL_GA = COL_U + POOL_WIDTH
COL_GP = COL_GA + D_MODEL

LANES = 128
MXU_TILE = 256
POOL_HALO = 16
CONV_HALO = 8
VMEM_LIMIT = 56 * 1024 * 1024

ROW_ALIGN = 768
TM_IN = 768
X_BLOCK = 256
TQ = 768
TK = MXU_TILE
SUM_ROWS = 16
TM_OUT = 704
FF_CHUNK = 256
DOWN_GROUP = 11


def _rms_scale(x):
    return lax.rsqrt(jnp.mean(x * x, axis=-1, keepdims=True) + EPS)


def _resident(shape):
    return pl.BlockSpec(shape, lambda *_: (0,) * len(shape), pipeline_mode=pl.Buffered(1))


def _mix_in_kernel(xf_ref, xa_ref, xb_ref, xc_ref, meta_ref, g_ref, w_in_ref, cos_ref, sin_ref,
                   wgrp_ref, pscale_ref, wpbr_ref,
                   h_ref, qq_ref, k_ref, vt_ref, sga_ref, pm_ref, ubuf_ref, *, seq_len):
    t = pl.program_id(1)
    tm = h_ref.shape[0]
    pos = t * tm + lax.broadcasted_iota(jnp.int32, (tm, 1), 0)

    h_ref[0:N_META, :] = jnp.where(t == 0, meta_ref[...], xf_ref[...])
    h_ref[N_META:N_META + X_BLOCK, :] = xa_ref[...]
    h_ref[N_META + X_BLOCK:N_META + 2 * X_BLOCK, :] = xb_ref[...]
    h_ref[N_META + 2 * X_BLOCK:tm, :] = xc_ref[0:tm - N_META - 2 * X_BLOCK, :]
    h = jnp.where(pos < seq_len, h_ref[...], 0.0)
    h_ref[...] = h
    hn = ((h * _rms_scale(h)) * g_ref[...]).astype(jnp.bfloat16)

    def proj(col, width):
        return jnp.dot(hn, w_in_ref[:, col:col + width], preferred_element_type=jnp.float32)

    cos = cos_ref[...]
    sin = sin_ref[...]
    lane = lax.broadcasted_iota(jnp.int32, (tm, LANES), 1)
    first_half = (lane % HEAD_DIM) < (HEAD_DIM // 2)

    def rope(pj, hh):
        xh = pj[:, hh * LANES:(hh + 1) * LANES]
        partner = jnp.where(first_half,
                            pltpu.roll(xh, LANES - HEAD_DIM // 2, axis=1),
                            pltpu.roll(xh, HEAD_DIM // 2, axis=1))
        return xh * cos + partner * sin

    @pl.when(t == 0)
    def _():
        ubuf_ref[0:POOL_HALO, :] = jnp.zeros((POOL_HALO, POOL_WIDTH), jnp.float32)

    ubuf_ref[POOL_HALO:POOL_HALO + tm, :] = proj(COL_U, POOL_WIDTH)

    pj = proj(COL_Q, QK_WIDTH)
    zeros = jnp.zeros((HEAD_DIM, LANES), qq_ref.dtype)
    for hh in range(N_HEADS):
        qt = (rope(pj, hh) * (LOG2E / math.sqrt(HEAD_DIM))).astype(qq_ref.dtype).T
        for blk in range(tm // LANES):
            src = slice(blk * LANES, (blk + 1) * LANES)
            c0 = slice(2 * blk * LANES, (2 * blk + 1) * LANES)
            c1 = slice((2 * blk + 1) * LANES, (2 * blk + 2) * LANES)
            qq_ref[hh, 0:HEAD_DIM, c0] = qt[0:HEAD_DIM, src]
            qq_ref[hh, HEAD_DIM:V_DIM, c0] = zeros
            qq_ref[hh, 0:HEAD_DIM, c1] = zeros
            qq_ref[hh, HEAD_DIM:V_DIM, c1] = qt[HEAD_DIM:V_DIM, src]

    mixed = []
    for g, w in enumerate(POOL_WINDOWS):
        cols = slice(g * POOL_GDIM, (g + 1) * POOL_GDIM)
        u_g = ubuf_ref[POOL_HALO:POOL_HALO + tm, cols]
        sums = u_g
        for j in range(1, w):
            sums = sums + ubuf_ref[POOL_HALO - j:POOL_HALO - j + tm, cols]
        count = jnp.minimum(pos + 1, w).astype(jnp.float32)
        pooled = (sums / count - u_g).astype(jnp.bfloat16)
        m_g = jnp.dot(pooled, wgrp_ref[g], preferred_element_type=jnp.float32)
        mixed.append((m_g * pscale_ref[:, cols]).astype(jnp.bfloat16))
    pool = jnp.concatenate(mixed, axis=-1)

    pj = proj(COL_K, QK_WIDTH)
    for hh in range(N_HEADS):
        k_ref[:, hh * LANES:(hh + 1) * LANES] = rope(pj, hh).astype(k_ref.dtype)
    pbr = jnp.dot(pool, wpbr_ref[...], preferred_element_type=jnp.float32)
    pm_ref[...] = (jax.nn.sigmoid(proj(COL_GP, D_MODEL)) * pbr).astype(pm_ref.dtype)
    sga_ref[...] = jax.nn.sigmoid(proj(COL_GA, D_MODEL)).astype(sga_ref.dtype)

    v = proj(COL_V, ATTN_WIDTH).astype(vt_ref.dtype)
    ones = jnp.ones((SUM_ROWS, TK), vt_ref.dtype)
    for hh in range(N_HEADS):
        for c in range(tm // TK):
            vt_ref[hh, c, 0:V_DIM, :] = v[c * TK:(c + 1) * TK, hh * LANES:(hh + 1) * LANES].T
            vt_ref[hh, c, V_DIM:V_DIM + SUM_ROWS, :] = ones

    ubuf_ref[0:POOL_HALO, :] = ubuf_ref[tm:tm + POOL_HALO, :]


def _mix_in(x, meta, L, g_mix, w_in, cos_t, sin_t, w_grp, pool_scale, w_pool_br):
    B, S, D = x.shape
    tm = TM_IN
    n_xblk = S // X_BLOCK
    assert L % tm == 0 and S % X_BLOCK == 0 and tm == 3 * X_BLOCK and X_BLOCK % N_META == 0
    assert tm % TK == 0 and tm % LANES == 0
    row = lambda width: pl.BlockSpec((None, tm, width), lambda b, t: (b, t, 0))
    x_head = pl.BlockSpec((None, N_META, D), lambda b, t: (b, jnp.maximum(t * (tm // N_META) - 1, 0), 0))
    x_body = lambda i: pl.BlockSpec((None, X_BLOCK, D),
                                    lambda b, t: (b, jnp.minimum(t * (tm // X_BLOCK) + i, n_xblk - 1), 0))
    bf = jnp.bfloat16
    return pl.pallas_call(
        functools.partial(_mix_in_kernel, seq_len=N_META + S),
        grid=(B, L // tm),
        in_specs=[
            x_head, x_body(0), x_body(1), x_body(2),
            _resident((N_META, D)),
            _resident((1, D)),
            _resident(w_in.shape),
            pl.BlockSpec((tm, LANES), lambda b, t: (t, 0)),
            pl.BlockSpec((tm, LANES), lambda b, t: (t, 0)),
            _resident(w_grp.shape),
            _resident((1, POOL_WIDTH)),
            _resident(w_pool_br.shape),
        ],
        out_specs=[
            row(D),
            pl.BlockSpec((None, N_HEADS, V_DIM, 2 * tm), lambda b, t: (b, 0, 0, t)),
            row(QK_WIDTH),
            pl.BlockSpec((None, N_HEADS, tm // TK, V_DIM + SUM_ROWS, TK), lambda b, t: (b, 0, t, 0, 0)),
            row(D),
            row(D),
        ],
        out_shape=[
            jax.ShapeDtypeStruct((B, L, D), jnp.float32),
            jax.ShapeDtypeStruct((B, N_HEADS, V_DIM, 2 * L), bf),
            jax.ShapeDtypeStruct((B, L, QK_WIDTH), bf),
            jax.ShapeDtypeStruct((B, N_HEADS, L // TK, V_DIM + SUM_ROWS, TK), bf),
            jax.ShapeDtypeStruct((B, L, D), bf),
            jax.ShapeDtypeStruct((B, L, D), bf),
        ],
        scratch_shapes=[pltpu.VMEM((tm + POOL_HALO, POOL_WIDTH), jnp.float32)],
        compiler_params=pltpu.CompilerParams(
            dimension_semantics=("arbitrary", "arbitrary"), vmem_limit_bytes=VMEM_LIMIT),
        name="mix_in",
    )(x, x, x, x, meta, g_mix, w_in, cos_t, sin_t, w_grp, pool_scale, w_pool_br)


def _diffattn_kernel(lam_ref, gsub_ref, qq_ref, k_ref, vt_ref, o_ref, s_ref, cmax_ref, m_ref, acc_ref,
                     *, last_tile_blocks):
    qi = pl.program_id(2)
    n_q = pl.num_programs(2)
    n_qblk = TQ // LANES
    diag_chunks = TQ // TK
    n_slots = diag_chunks

    def run(n_blk):
        hi = 2 * LANES * n_blk
        m_ref[...] = jnp.full(m_ref.shape, -jnp.inf, jnp.float32)
        acc_ref[...] = jnp.zeros(acc_ref.shape, jnp.float32)

        def scores(j, slot, diag=None):
            lo = 0 if diag is None else diag * 2 * TK
            kc = k_ref[pl.ds(pl.multiple_of(j * TK, TK), TK), :]
            s = jnp.dot(kc, qq_ref[:, lo:hi], preferred_element_type=jnp.float32)
            if diag is not None:
                part = min(2 * TK, hi - lo)
                key = lax.broadcasted_iota(jnp.int32, (TK, part), 0)
                lane = lax.broadcasted_iota(jnp.int32, (TK, part), 1)
                visible = key <= (lane // (2 * LANES)) * LANES + lane % LANES
                masked = jnp.where(visible, s[:, 0:part], -jnp.inf)
                s = jnp.concatenate([masked, s[:, part:]], axis=1) if hi - lo > part else masked
            s_ref[slot, :, lo:hi] = s
            cmax_ref[slot, :, lo:hi] = jnp.max(s, axis=0, keepdims=True)

        def consume(j, slot, lo=0):
            m = m_ref[:, lo:hi]
            m_new = jnp.maximum(m, cmax_ref[slot, :, lo:hi])
            alpha = jnp.exp2(m - m_new)
            p = jnp.exp2(s_ref[slot, :, lo:hi] - m_new)
            pv = jnp.dot(vt_ref[j], p.astype(jnp.bfloat16), preferred_element_type=jnp.float32)
            acc_ref[:, lo:hi] = alpha * acc_ref[:, lo:hi] + pv
            m_ref[:, lo:hi] = m_new

        scores(0, 0)

        def block(base):
            for r in range(n_slots):
                scores(base + r + 1, (r + 1) % n_slots)
                consume(base + r, r)

        def trio(i, c):
            for b in range(3):
                block((3 * i + b) * n_slots)
            return c

        lax.fori_loop(0, qi // 3, trio, 0)

        @pl.when(qi % 3 == 1)
        def _():
            block((qi - 1) * n_slots)

        @pl.when(qi % 3 == 2)
        def _():
            block((qi - 2) * n_slots)
            block((qi - 1) * n_slots)

        base = qi * n_slots
        live = [d for d in range(diag_chunks) if d * 2 * TK < hi]
        scores(base, 0, diag=0)
        for d in live:
            if d + 1 in live:
                scores(base + d + 1, d + 1, diag=d + 1)
            consume(base + d, d, lo=d * 2 * TK)

        lam = lam_ref[...]
        lam_full = (jnp.exp(jnp.sum(lam[0:1, :] * lam[1:2, :], axis=-1, keepdims=True))
                    - jnp.exp(jnp.sum(lam[2:3, :] * lam[3:4, :], axis=-1, keepdims=True)) + LAM_INIT)
        gsub = gsub_ref[...] * (1.0 - LAM_INIT)
        for blk in range(n_qblk):
            rows = slice(blk * LANES, (blk + 1) * LANES)
            if blk >= n_blk:
                o_ref[rows, :] = jnp.zeros((LANES, V_DIM), o_ref.dtype)
                continue
            c0 = slice(2 * blk * LANES, (2 * blk + 1) * LANES)
            c1 = slice((2 * blk + 1) * LANES, (2 * blk + 2) * LANES)
            o = (acc_ref[0:V_DIM, c0] / acc_ref[V_DIM:V_DIM + 1, c0]
                 - lam_full * (acc_ref[0:V_DIM, c1] / acc_ref[V_DIM:V_DIM + 1, c1]))
            o = o * lax.rsqrt(jnp.mean(o * o, axis=0, keepdims=True) + EPS) * gsub
            o_ref[rows, :] = o.T.astype(o_ref.dtype)

    if last_tile_blocks == n_qblk:
        run(n_qblk)
    else:
        pl.when(qi < n_q - 1)(lambda: run(n_qblk))
        pl.when(qi == n_q - 1)(lambda: run(last_tile_blocks))


def _diffattn(lam, g_subln, qq, k, vt, seq_len):
    B, L, _ = k.shape
    assert TQ % TK == 0 and L % TQ == 0 and TQ % LANES == 0 and L - TQ < seq_len <= L
    last_tile_blocks = pl.cdiv(seq_len - (L - TQ), LANES)
    return pl.pallas_call(
        functools.partial(_diffattn_kernel, last_tile_blocks=last_tile_blocks),
        grid=(B, N_HEADS, L // TQ),
        in_specs=[
            _resident(lam.shape),
            _resident((V_DIM, 1)),
            pl.BlockSpec((None, None, V_DIM, 2 * TQ), lambda b, h, i: (b, h, 0, i)),
            pl.BlockSpec((None, L, V_DIM), lambda b, h, i: (b, 0, h)),
            pl.BlockSpec((None, None, L // TK, V_DIM + SUM_ROWS, TK), lambda b, h, i: (b, h, 0, 0, 0)),
        ],
        out_specs=pl.BlockSpec((None, TQ, V_DIM), lambda b, h, i: (b, i, h)),
        out_shape=jax.ShapeDtypeStruct((B, L, ATTN_WIDTH), jnp.bfloat16),
        scratch_shapes=[pltpu.VMEM((TQ // TK, TK, 2 * TQ), jnp.float32),
                        pltpu.VMEM((TQ // TK, 1, 2 * TQ), jnp.float32),
                        pltpu.VMEM((1, 2 * TQ), jnp.float32),
                        pltpu.VMEM((V_DIM + SUM_ROWS, 2 * TQ), jnp.float32)],
        compiler_params=pltpu.CompilerParams(
            dimension_semantics=("arbitrary", "arbitrary", "arbitrary"), vmem_limit_bytes=VMEM_LIMIT),
        name="diffattn",
    )(lam, g_subln, qq, k, vt)


def _mix_out_kernel(h_ref, attn_ref, sga_ref, pm_ref, wabr_ref, wout_ref, gffn_ref, wup_ref,
                    convw_ref, convb_ref, wdown_ref, gfin_ref, o_ref, halo_ref, prev_ref):
    t = pl.program_id(1)
    n_tiles = pl.num_programs(1) - 1
    tm = h_ref.shape[0]
    ch = FF_CHUNK

    def emit(first_rows_of_next):
        o_ref[0:tm - N_META, :] = prev_ref[N_META:tm, :]
        o_ref[tm - N_META:tm, :] = first_rows_of_next

    @pl.when(t == n_tiles)
    def _():
        emit(jnp.zeros((N_META, D_MODEL), o_ref.dtype))

    @pl.when(t < n_tiles)
    def _():
        abr = jnp.dot(attn_ref[...], wabr_ref[...], preferred_element_type=jnp.float32)
        merged = sga_ref[...].astype(jnp.float32) * abr + pm_ref[...].astype(jnp.float32)
        h1 = h_ref[...] + jnp.dot(merged.astype(jnp.bfloat16), wout_ref[...],
                                  preferred_element_type=jnp.float32)
        hn = ((h1 * _rms_scale(h1)) * gffn_ref[...]).astype(jnp.bfloat16)

        @pl.when(t == 0)
        def _():
            halo_ref[...] = jnp.zeros_like(halo_ref)

        def up_proj(c):
            return tuple(jnp.dot(hn, wup_ref[:, col:col + ch], preferred_element_type=jnp.float32)
                         for col in (c * ch, D_FF + c * ch))

        def conv(up, col):
            taps = [convw_ref[CONV_WIDTH - 1 - j:CONV_WIDTH - j, col:col + ch] for j in range(CONV_WIDTH)]
            bias = convb_ref[:, col:col + ch]
            y = bias + taps[0] * up
            for j in range(1, CONV_WIDTH):
                y = y + taps[j] * pltpu.roll(up, j, axis=0)
            ext = jnp.concatenate([halo_ref[:, col:col + ch], up[0:CONV_HALO, :]], axis=0)
            y0 = bias + taps[0] * ext[CONV_HALO:2 * CONV_HALO, :]
            for j in range(1, CONV_WIDTH):
                y0 = y0 + taps[j] * ext[CONV_HALO - j:2 * CONV_HALO - j, :]
            halo_ref[:, col:col + ch] = up[tm - CONV_HALO:tm, :]
            return jnp.concatenate([y0, y[CONV_HALO:, :]], axis=0)

        n_chunks = D_FF // ch
        acc = h1
        up = up_proj(0)
        acts = []
        for c in range(n_chunks):
            up_next = up_proj(c + 1) if c + 1 < n_chunks else None
            val = conv(up[0], c * ch)
            gate = conv(up[1], D_FF + c * ch)
            acts.append((jax.nn.silu(gate) * val).astype(jnp.bfloat16))
            if len(acts) == DOWN_GROUP or c + 1 == n_chunks:
                lo = (c + 1 - len(acts)) * ch
                act = jnp.concatenate(acts, axis=-1) if len(acts) > 1 else acts[0]
                acc = jnp.concatenate(
                    [acc[r, :] + jnp.dot(act[r, :], wdown_ref[lo:(c + 1) * ch, :], preferred_element_type=jnp.float32)
                     for r in (slice(0, tm // 2), slice(tm // 2, tm))], axis=0)
                acts = []
            up = up_next

        h2 = acc
        res = ((h2 * _rms_scale(h2)) * gfin_ref[...]).astype(o_ref.dtype)

        @pl.when(t > 0)
        def _():
            emit(res[0:N_META, :])

        prev_ref[...] = res


def _mix_out(h, attn, sga, pm, w_attn_br, w_out, g_ffn, w_up, conv_w, conv_b, w_down, g_final, seq):
    B, L, D = h.shape
    tm = TM_OUT
    n_tiles = L // tm
    assert D_FF % FF_CHUNK == 0 and L % tm == 0 and pl.cdiv(seq, tm) == n_tiles and tm > N_META
    row = pl.BlockSpec((None, tm, D), lambda b, t: (b, jnp.minimum(t, n_tiles - 1), 0))
    return pl.pallas_call(
        _mix_out_kernel,
        grid=(B, n_tiles + 1),
        in_specs=[
            row, row, row, row,
            _resident(w_attn_br.shape),
            _resident(w_out.shape),
            _resident((1, D)),
            _resident(w_up.shape),
            _resident(conv_w.shape),
            _resident(conv_b.shape),
            _resident(w_down.shape),
            _resident((1, D)),
        ],
        out_specs=pl.BlockSpec((None, tm, D), lambda b, t: (b, jnp.maximum(t - 1, 0), 0)),
        out_shape=jax.ShapeDtypeStruct((B, seq, D), jnp.float32),
        scratch_shapes=[pltpu.VMEM((CONV_HALO, 2 * D_FF), jnp.float32),
                        pltpu.VMEM((tm, D), jnp.float32)],
        compiler_params=pltpu.CompilerParams(
            dimension_semantics=("arbitrary", "arbitrary"), vmem_limit_bytes=VMEM_LIMIT),
        name="mix_out",
    )(h, attn, sga, pm, w_attn_br, w_out, g_ffn, w_up, conv_w, conv_b, w_down, g_final)


def _rope_tables(length):
    pos = np.arange(length, dtype=np.float32)
    inv = (1.0 / (np.float32(ROPE_THETA) ** (np.arange(0, HEAD_DIM, 2, dtype=np.float32) / np.float32(HEAD_DIM))))
    ang = pos[:, None] * inv.astype(np.float32)[None, :]
    cos, sin = np.cos(ang), np.sin(ang)
    cos_t = np.concatenate([cos, cos, cos, cos], axis=-1)
    sin_t = np.concatenate([-sin, sin, -sin, sin], axis=-1)
    return jnp.asarray(cos_t, jnp.float32), jnp.asarray(sin_t, jnp.float32)


def kernel(x, meta_tokens, g_mix, w_in, lam, g_subln, w_pool_grp, pool_scale, w_attn_br, w_pool_br,
           w_out, g_ffn, w_up, conv_w, conv_b, w_down, g_final):
    B, S, D = x.shape
    L = N_META + S
    L_pad = ((L + ROW_ALIGN - 1) // ROW_ALIGN) * ROW_ALIGN
    bf = jnp.bfloat16
    cos_t, sin_t = _rope_tables(L_pad)

    h, qq, k, vt, sga, pm = _mix_in(x, meta_tokens.astype(x.dtype), L_pad, g_mix[0][None], w_in[0].astype(bf),
                                    cos_t, sin_t, w_pool_grp[0].astype(bf), pool_scale[0][None],
                                    w_pool_br[0].astype(bf))
    attn = _diffattn(lam[0], g_subln[0][:, None], qq, k, vt, L)
    return _mix_out(h, attn, sga, pm, w_attn_br[0].astype(bf), w_out[0].astype(bf), g_ffn[0][None],
                    w_up[0].astype(bf), conv_w[0], conv_b[0][None], w_down[0].astype(bf), g_final[None], S)
```

```python
import functools
import math

import jax
import jax.numpy as jnp
import numpy as np
from jax import lax
from jax.experimental import pallas as pl
from jax.experimental.pallas import tpu as pltpu

D_MODEL = 1024
N_META = 16
N_HEADS = 8
HEAD_DIM = 64
V_DIM = 2 * HEAD_DIM
QK_WIDTH = N_HEADS * 2 * HEAD_DIM
ATTN_WIDTH = N_HEADS * V_DIM
POOL_GROUPS = 4
POOL_WINDOWS = (2, 4, 8, 16)
POOL_WIDTH = 512
POOL_GDIM = POOL_WIDTH // POOL_GROUPS
D_FF = 2816
CONV_WIDTH = 3
ROPE_THETA = 10000.0
EPS = 1e-6
LAM_INIT = 0.8 - 0.6 * math.exp(-0.3 * 0)
LOG2E = math.log2(math.e)

COL_Q = 0
COL_K = QK_WIDTH
COL_V = 2 * QK_WIDTH
COL_U = 2 * QK_WIDTH + ATTN_WIDTH
COL_GA = COL_U + POOL_WIDTH
COL_GP = COL_GA + D_MODEL

LANES = 128
MXU_TILE = 256
POOL_HALO = 16
CONV_HALO = 8
VMEM_LIMIT = 56 * 1024 * 1024

ROW_ALIGN = 768
TM_IN = 768
X_BLOCK = 256
TQ = 768
TK = MXU_TILE
SUM_ROWS = 16
TM_OUT = 704
FF_CHUNK = 256
DOWN_GROUP = 11


def _rms_scale(x):
    return lax.rsqrt(jnp.mean(x * x, axis=-1, keepdims=True) + EPS)


def _resident(shape):
    return pl.BlockSpec(shape, lambda *_: (0,) * len(shape), pipeline_mode=pl.Buffered(1))


def _mix_in_kernel(xf_ref, xa_ref, xb_ref, xc_ref, meta_ref, g_ref, w_in_ref, cos_ref, sin_ref,
                   wgrp_ref, pscale_ref, wpbr_ref,
                   h_ref, qq_ref, k_ref, vt_ref, sga_ref, pm_ref, ubuf_ref, *, seq_len):
    t = pl.program_id(1)
    tm = h_ref.shape[0]
    pos = t * tm + lax.broadcasted_iota(jnp.int32, (tm, 1), 0)

    h_ref[0:N_META, :] = jnp.where(t == 0, meta_ref[...], xf_ref[...])
    h_ref[N_META:N_META + X_BLOCK, :] = xa_ref[...]
    h_ref[N_META + X_BLOCK:N_META + 2 * X_BLOCK, :] = xb_ref[...]
    h_ref[N_META + 2 * X_BLOCK:tm, :] = xc_ref[0:tm - N_META - 2 * X_BLOCK, :]
    h = jnp.where(pos < seq_len, h_ref[...], 0.0)
    h_ref[...] = h
    hn = ((h * _rms_scale(h)) * g_ref[...]).astype(jnp.bfloat16)

    def proj(col, width):
        return jnp.dot(hn, w_in_ref[:, col:col + width], preferred_element_type=jnp.float32)

    cos = cos_ref[...]
    sin = sin_ref[...]
    lane = lax.broadcasted_iota(jnp.int32, (tm, LANES), 1)
    first_half = (lane % HEAD_DIM) < (HEAD_DIM // 2)

    def rope(pj, hh):
        xh = pj[:, hh * LANES:(hh + 1) * LANES]
        partner = jnp.where(first_half,
                            pltpu.roll(xh, LANES - HEAD_DIM // 2, axis=1),
                            pltpu.roll(xh, HEAD_DIM // 2, axis=1))
        return xh * cos + partner * sin

    @pl.when(t == 0)
    def _():
        ubuf_ref[0:POOL_HALO, :] = jnp.zeros((POOL_HALO, POOL_WIDTH), jnp.float32)

    ubuf_ref[POOL_HALO:POOL_HALO + tm, :] = proj(COL_U, POOL_WIDTH)

    pj = proj(COL_Q, QK_WIDTH)
    zeros = jnp.zeros((HEAD_DIM, LANES), qq_ref.dtype)
    for hh in range(N_HEADS):
        qt = (rope(pj, hh) * (LOG2E / math.sqrt(HEAD_DIM))).astype(qq_ref.dtype).T
        for blk in range(tm // LANES):
            src = slice(blk * LANES, (blk + 1) * LANES)
            c0 = slice(2 * blk * LANES, (2 * blk + 1) * LANES)
            c1 = slice((2 * blk + 1) * LANES, (2 * blk + 2) * LANES)
            qq_ref[hh, 0:HEAD_DIM, c0] = qt[0:HEAD_DIM, src]
            qq_ref[hh, HEAD_DIM:V_DIM, c0] = zeros
            qq_ref[hh, 0:HEAD_DIM, c1] = zeros
            qq_ref[hh, HEAD_DIM:V_DIM, c1] = qt[HEAD_DIM:V_DIM, src]

    mixed = []
    for g, w in enumerate(POOL_WINDOWS):
        cols = slice(g * POOL_GDIM, (g + 1) * POOL_GDIM)
        u_g = ubuf_ref[POOL_HALO:POOL_HALO + tm, cols]
        sums = u_g
        for j in range(1, w):
            sums = sums + ubuf_ref[POOL_HALO - j:POOL_HALO - j + tm, cols]
        count = jnp.minimum(pos + 1, w).astype(jnp.float32)
        pooled = (sums / count - u_g).astype(jnp.bfloat16)
        m_g = jnp.dot(pooled, wgrp_ref[g], preferred_element_type=jnp.float32)
        mixed.append((m_g * pscale_ref[:, cols]).astype(jnp.bfloat16))
    pool = jnp.concatenate(mixed, axis=-1)

    pj = proj(COL_K, QK_WIDTH)
    for hh in range(N_HEADS):
        k_ref[:, hh * LANES:(hh + 1) * LANES] = rope(pj, hh).astype(k_ref.dtype)
    pbr = jnp.dot(pool, wpbr_ref[...], preferred_element_type=jnp.float32)

    v = proj(COL_V, ATTN_WIDTH).astype(vt_ref.dtype)
    ones = jnp.ones((SUM_ROWS, TK), vt_ref.dtype)
    for hh in range(N_HEADS):
        for c in range(tm // TK):
            vt_ref[hh, c, 0:V_DIM, :] = v[c * TK:(c + 1) * TK, hh * LANES:(hh + 1) * LANES].T
            vt_ref[hh, c, V_DIM:V_DIM + SUM_ROWS, :] = ones
    pm_ref[...] = (jax.nn.sigmoid(proj(COL_GP, D_MODEL)) * pbr).astype(pm_ref.dtype)
    sga_ref[...] = jax.nn.sigmoid(proj(COL_GA, D_MODEL)).astype(sga_ref.dtype)

    ubuf_ref[0:POOL_HALO, :] = ubuf_ref[tm:tm + POOL_HALO, :]


def _mix_in(x, meta, L, g_mix, w_in, cos_t, sin_t, w_grp, pool_scale, w_pool_br):
    B, S, D = x.shape
    tm = TM_IN
    n_xblk = S // X_BLOCK
    assert L % tm == 0 and S % X_BLOCK == 0 and tm == 3 * X_BLOCK and X_BLOCK % N_META == 0
    assert tm % TK == 0 and tm % LANES == 0
    row = lambda width: pl.BlockSpec((None, tm, width), lambda b, t: (b, t, 0))
    x_head = pl.BlockSpec((None, N_META, D), lambda b, t: (b, jnp.maximum(t * (tm // N_META) - 1, 0), 0))
    x_body = lambda i: pl.BlockSpec((None, X_BLOCK, D),
                                    lambda b, t: (b, jnp.minimum(t * (tm // X_BLOCK) + i, n_xblk - 1), 0))
    bf = jnp.bfloat16
    return pl.pallas_call(
        functools.partial(_mix_in_kernel, seq_len=N_META + S),
        grid=(B, L // tm),
        in_specs=[
            x_head, x_body(0), x_body(1), x_body(2),
            _resident((N_META, D)),
            _resident((1, D)),
            _resident(w_in.shape),
            pl.BlockSpec((tm, LANES), lambda b, t: (t, 0)),
            pl.BlockSpec((tm, LANES), lambda b, t: (t, 0)),
            _resident(w_grp.shape),
            _resident((1, POOL_WIDTH)),
            _resident(w_pool_br.shape),
        ],
        out_specs=[
            row(D),
            pl.BlockSpec((None, N_HEADS, V_DIM, 2 * tm), lambda b, t: (b, 0, 0, t)),
            row(QK_WIDTH),
            pl.BlockSpec((None, N_HEADS, tm // TK, V_DIM + SUM_ROWS, TK), lambda b, t: (b, 0, t, 0, 0)),
            row(D),
            row(D),
        ],
        out_shape=[
            jax.ShapeDtypeStruct((B, L, D), jnp.float32),
            jax.ShapeDtypeStruct((B, N_HEADS, V_DIM, 2 * L), bf),
            jax.ShapeDtypeStruct((B, L, QK_WIDTH), bf),
            jax.ShapeDtypeStruct((B, N_HEADS, L // TK, V_DIM + SUM_ROWS, TK), bf),
            jax.ShapeDtypeStruct((B, L, D), bf),
            jax.ShapeDtypeStruct((B, L, D), bf),
        ],
        scratch_shapes=[pltpu.VMEM((tm + POOL_HALO, POOL_WIDTH), jnp.float32)],
        compiler_params=pltpu.CompilerParams(
            dimension_semantics=("arbitrary", "arbitrary"), vmem_limit_bytes=VMEM_LIMIT),
        name="mix_in",
    )(x, x, x, x, meta, g_mix, w_in, cos_t, sin_t, w_grp, pool_scale, w_pool_br)


def _diffattn_kernel(lam_ref, gsub_ref, qq_ref, k_ref, vt_ref, o_ref, s_ref, cmax_ref, m_ref, acc_ref,
                     *, last_tile_blocks):
    qi = pl.program_id(2)
    n_q = pl.num_programs(2)
    n_qblk = TQ // LANES
    diag_chunks = TQ // TK
    n_slots = diag_chunks

    def run(n_blk):
        hi = 2 * LANES * n_blk
        m_ref[...] = jnp.full(m_ref.shape, -jnp.inf, jnp.float32)
        acc_ref[...] = jnp.zeros(acc_ref.shape, jnp.float32)

        def scores(j, slot, diag=None):
            lo = 0 if diag is None else diag * 2 * TK
            kc = k_ref[pl.ds(pl.multiple_of(j * TK, TK), TK), :]
            s = jnp.dot(kc, qq_ref[:, lo:hi], preferred_element_type=jnp.float32)
            if diag is not None:
                part = min(2 * TK, hi - lo)
                key = lax.broadcasted_iota(jnp.int32, (TK, part), 0)
                lane = lax.broadcasted_iota(jnp.int32, (TK, part), 1)
                visible = key <= (lane // (2 * LANES)) * LANES + lane % LANES
                masked = jnp.where(visible, s[:, 0:part], -jnp.inf)
                s = jnp.concatenate([masked, s[:, part:]], axis=1) if hi - lo > part else masked
            s_ref[slot, :, lo:hi] = s
            cmax_ref[slot, :, lo:hi] = jnp.max(s, axis=0, keepdims=True)

        def consume(j, slot, lo=0):
            m = m_ref[:, lo:hi]
            m_new = jnp.maximum(m, cmax_ref[slot, :, lo:hi])
            alpha = jnp.exp2(m - m_new)
            p = jnp.exp2(s_ref[slot, :, lo:hi] - m_new)
            pv = jnp.dot(vt_ref[j], p.astype(jnp.bfloat16), preferred_element_type=jnp.float32)
            acc_ref[:, lo:hi] = alpha * acc_ref[:, lo:hi] + pv
            m_ref[:, lo:hi] = m_new

        scores(0, 0)

        def block(base):
            for r in range(n_slots):
                scores(base + r + 1, (r + 1) % n_slots)
                consume(base + r, r)

        def trio(i, c):
            for b in range(3):
                block((3 * i + b) * n_slots)
            return c

        lax.fori_loop(0, qi // 3, trio, 0)

        @pl.when(qi % 3 == 1)
        def _():
            block((qi - 1) * n_slots)

        @pl.when(qi % 3 == 2)
        def _():
            block((qi - 2) * n_slots)
            block((qi - 1) * n_slots)

        base = qi * n_slots
        live = [d for d in range(diag_chunks) if d * 2 * TK < hi]
        scores(base, 0, diag=0)
        for d in live:
            if d + 1 in live:
                scores(base + d + 1, d + 1, diag=d + 1)
            consume(base + d, d, lo=d * 2 * TK)

        lam = lam_ref[...]
        lam_full = (jnp.exp(jnp.sum(lam[0:1, :] * lam[1:2, :], axis=-1, keepdims=True))
                    - jnp.exp(jnp.sum(lam[2:3, :] * lam[3:4, :], axis=-1, keepdims=True)) + LAM_INIT)
        gsub = gsub_ref[...] * (1.0 - LAM_INIT)
        for blk in range(n_qblk):
            rows = slice(blk * LANES, (blk + 1) * LANES)
            if blk >= n_blk:
                o_ref[rows, :] = jnp.zeros((LANES, V_DIM), o_ref.dtype)
                continue
            c0 = slice(2 * blk * LANES, (2 * blk + 1) * LANES)
            c1 = slice((2 * blk + 1) * LANES, (2 * blk + 2) * LANES)
            o = (acc_ref[0:V_DIM, c0] / acc_ref[V_DIM:V_DIM + 1, c0]
                 - lam_full * (acc_ref[0:V_DIM, c1] / acc_ref[V_DIM:V_DIM + 1, c1]))
            o = o * lax.rsqrt(jnp.mean(o * o, axis=0, keepdims=True) + EPS) * gsub
            o_ref[rows, :] = o.T.astype(o_ref.dtype)

    if last_tile_blocks == n_qblk:
        run(n_qblk)
    else:
        pl.when(qi < n_q - 1)(lambda: run(n_qblk))
        pl.when(qi == n_q - 1)(lambda: run(last_tile_blocks))


def _diffattn(lam, g_subln, qq, k, vt, seq_len):
    B, L, _ = k.shape
    assert TQ % TK == 0 and L % TQ == 0 and TQ % LANES == 0 and L - TQ < seq_len <= L
    last_tile_blocks = pl.cdiv(seq_len - (L - TQ), LANES)
    return pl.pallas_call(
        functools.partial(_diffattn_kernel, last_tile_blocks=last_tile_blocks),
        grid=(B, N_HEADS, L // TQ),
        in_specs=[
            _resident(lam.shape),
            _resident((V_DIM, 1)),
            pl.BlockSpec((None, None, V_DIM, 2 * TQ), lambda b, h, i: (b, h, 0, i)),
            pl.BlockSpec((None, L, V_DIM), lambda b, h, i: (b, 0, h)),
            pl.BlockSpec((None, None, L // TK, V_DIM + SUM_ROWS, TK), lambda b, h, i: (b, h, 0, 0, 0)),
        ],
        out_specs=pl.BlockSpec((None, TQ, V_DIM), lambda b, h, i: (b, i, h)),
        out_shape=jax.ShapeDtypeStruct((B, L, ATTN_WIDTH), jnp.bfloat16),
        scratch_shapes=[pltpu.VMEM((TQ // TK, TK, 2 * TQ), jnp.float32),
                        pltpu.VMEM((TQ // TK, 1, 2 * TQ), jnp.float32),
                        pltpu.VMEM((1, 2 * TQ), jnp.float32),
                        pltpu.VMEM((V_DIM + SUM_ROWS, 2 * TQ), jnp.float32)],
        compiler_params=pltpu.CompilerParams(
            dimension_semantics=("parallel", "parallel", "arbitrary"), vmem_limit_bytes=VMEM_LIMIT),
        name="diffattn",
    )(lam, g_subln, qq, k, vt)


def _mix_out_kernel(h_ref, attn_ref, sga_ref, pm_ref, wabr_ref, wout_ref, gffn_ref, wup_ref,
                    convw_ref, convb_ref, wdown_ref, gfin_ref, o_ref, halo_ref, prev_ref):
    t = pl.program_id(1)
    n_tiles = pl.num_programs(1) - 1
    tm = h_ref.shape[0]
    ch = FF_CHUNK

    def emit(first_rows_of_next):
        o_ref[0:tm - N_META, :] = prev_ref[N_META:tm, :]
        o_ref[tm - N_META:tm, :] = first_rows_of_next

    @pl.when(t == n_tiles)
    def _():
        emit(jnp.zeros((N_META, D_MODEL), o_ref.dtype))

    @pl.when(t < n_tiles)
    def _():
        abr = jnp.dot(attn_ref[...], wabr_ref[...], preferred_element_type=jnp.float32)
        merged = sga_ref[...].astype(jnp.float32) * abr + pm_ref[...].astype(jnp.float32)
        h1 = h_ref[...] + jnp.dot(merged.astype(jnp.bfloat16), wout_ref[...],
                                  preferred_element_type=jnp.float32)
        hn = ((h1 * _rms_scale(h1)) * gffn_ref[...]).astype(jnp.bfloat16)

        @pl.when(t == 0)
        def _():
            halo_ref[...] = jnp.zeros_like(halo_ref)

        def up_proj(c):
            return tuple(jnp.dot(hn, wup_ref[:, col:col + ch], preferred_element_type=jnp.float32)
                         for col in (c * ch, D_FF + c * ch))

        def conv(up, col):
            taps = [convw_ref[CONV_WIDTH - 1 - j:CONV_WIDTH - j, col:col + ch] for j in range(CONV_WIDTH)]
            bias = convb_ref[:, col:col + ch]
            y = bias + taps[0] * up
            for j in range(1, CONV_WIDTH):
                y = y + taps[j] * pltpu.roll(up, j, axis=0)
            ext = jnp.concatenate([halo_ref[:, col:col + ch], up[0:CONV_HALO, :]], axis=0)
            y0 = bias + taps[0] * ext[CONV_HALO:2 * CONV_HALO, :]
            for j in range(1, CONV_WIDTH):
                y0 = y0 + taps[j] * ext[CONV_HALO - j:2 * CONV_HALO - j, :]
            halo_ref[:, col:col + ch] = up[tm - CONV_HALO:tm, :]
            return jnp.concatenate([y0, y[CONV_HALO:, :]], axis=0)

        n_chunks = D_FF // ch
        acc = h1
        up = up_proj(0)
        acts = []
        for c in range(n_chunks):
            up_next = up_proj(c + 1) if c + 1 < n_chunks else None
            val = conv(up[0], c * ch)
            gate = conv(up[1], D_FF + c * ch)
            acts.append((jax.nn.silu(gate) * val).astype(jnp.bfloat16))
            if len(acts) == DOWN_GROUP or c + 1 == n_chunks:
                lo = (c + 1 - len(acts)) * ch
                act = jnp.concatenate(acts, axis=-1) if len(acts) > 1 else acts[0]
                acc = acc + jnp.dot(act, wdown_ref[lo:(c + 1) * ch, :], preferred_element_type=jnp.float32)
                acts = []
            up = up_next

        h2 = acc
        res = ((h2 * _rms_scale(h2)) * gfin_ref[...]).astype(o_ref.dtype)

        @pl.when(t > 0)
        def _():
            emit(res[0:N_META, :])

        prev_ref[...] = res


def _mix_out(h, attn, sga, pm, w_attn_br, w_out, g_ffn, w_up, conv_w, conv_b, w_down, g_final, seq):
    B, L, D = h.shape
    tm = TM_OUT
    n_tiles = L // tm
    assert D_FF % FF_CHUNK == 0 and L % tm == 0 and pl.cdiv(seq, tm) == n_tiles and tm > N_META
    row = pl.BlockSpec((None, tm, D), lambda b, t: (b, jnp.minimum(t, n_tiles - 1), 0))
    return pl.pallas_call(
        _mix_out_kernel,
        grid=(B, n_tiles + 1),
        in_specs=[
            row, row, row, row,
            _resident(w_attn_br.shape),
            _resident(w_out.shape),
            _resident((1, D)),
            _resident(w_up.shape),
            _resident(conv_w.shape),
            _resident(conv_b.shape),
            _resident(w_down.shape),
            _resident((1, D)),
        ],
        out_specs=pl.BlockSpec((None, tm, D), lambda b, t: (b, jnp.maximum(t - 1, 0), 0)),
        out_shape=jax.ShapeDtypeStruct((B, seq, D), jnp.float32),
        scratch_shapes=[pltpu.VMEM((CONV_HALO, 2 * D_FF), jnp.float32),
                        pltpu.VMEM((tm, D), jnp.float32)],
        compiler_params=pltpu.CompilerParams(
            dimension_semantics=("arbitrary", "arbitrary"), vmem_limit_bytes=VMEM_LIMIT),
        name="mix_out",
    )(h, attn, sga, pm, w_attn_br, w_out, g_ffn, w_up, conv_w, conv_b, w_down, g_final)


def _rope_tables(length):
    pos = np.arange(length, dtype=np.float32)
    inv = (1.0 / (np.float32(ROPE_THETA) ** (np.arange(0, HEAD_DIM, 2, dtype=np.float32) / np.float32(HEAD_DIM))))
    ang = pos[:, None] * inv.astype(np.float32)[None, :]
    cos, sin = np.cos(ang), np.sin(ang)
    cos_t = np.concatenate([cos, cos, cos, cos], axis=-1)
    sin_t = np.concatenate([-sin, sin, -sin, sin], axis=-1)
    return jnp.asarray(cos_t, jnp.float32), jnp.asarray(sin_t, jnp.float32)


def kernel(x, meta_tokens, g_mix, w_in, lam, g_subln, w_pool_grp, pool_scale, w_attn_br, w_pool_br,
           w_out, g_ffn, w_up, conv_w, conv_b, w_down, g_final):
    B, S, D = x.shape
    L = N_META + S
    L_pad = ((L + ROW_ALIGN - 1) // ROW_ALIGN) * ROW_ALIGN
    bf = jnp.bfloat16
    cos_t, sin_t = _rope_tables(L_pad)

    h, qq, k, vt, sga, pm = _mix_in(x, meta_tokens.astype(x.dtype), L_pad, g_mix[0][None], w_in[0].astype(bf),
                                    cos_t, sin_t, w_pool_grp[0].astype(bf), pool_scale[0][None],
                                    w_pool_br[0].astype(bf))
    attn = _diffattn(lam[0], g_subln[0][:, None], qq, k, vt, L)
    return _mix_out(h, attn, sga, pm, w_attn_br[0].astype(bf), w_out[0].astype(bf), g_ffn[0][None],
                    w_up[0].astype(bf), conv_w[0], conv_b[0][None], w_down[0].astype(bf), g_final[None], S)
```
